```python
import math
import jax
import jax.numpy as jnp
from jax import lax
import numpy as np

D_MODEL = 2048
BATCH = 8
SEQ = 4096
DEPTH = 4

CHUNK = 64
Q_BLOCK = 128
NORM_EPS = 1e-6
ROPE_THETA = 10000.0

FOX_HEADS = 6
FOX_DH = 128
FOX_W = FOX_HEADS * FOX_DH
FORGET_BIAS_CENTER = 3.0

MLA_HEADS = 6
MLA_NOPE = 128
MLA_ROPE = 64
MLA_V = 128
MLA_Q_LORA = 512
MLA_KV_LORA = 256
MLA_W = MLA_HEADS * MLA_V

RET_HEADS = 4
RET_DK = 128
RET_DV = 256
RET_QK_W = RET_HEADS * RET_DK
RET_V_W = RET_HEADS * RET_DV

N_BRANCH = 3

D_FF = 5632
CONV_W = 3

IN_SPLITS = (FOX_W, FOX_W, FOX_W, FOX_HEADS,
             MLA_Q_LORA, MLA_KV_LORA, MLA_ROPE,
             RET_QK_W, RET_QK_W, RET_V_W, RET_V_W,
             N_BRANCH * D_MODEL)
IN_WIDTH = sum(IN_SPLITS)

kernel_name = 'hybrid_fox_mla_retention_convffn'


def rms_norm(x, g):
    xf = x.astype(jnp.float32)
    y = xf * lax.rsqrt(jnp.mean(xf * xf, axis=-1, keepdims=True) + NORM_EPS)
    return (y * g.astype(jnp.float32)).astype(x.dtype)


def apply_rope(x):
    s, d = x.shape[1], x.shape[-1]
    pos = jnp.arange(s, dtype=jnp.float32)
    inv_freq = ROPE_THETA ** (-jnp.arange(0, d, 2, dtype=jnp.float32) / d)
    ang = pos[:, None] * inv_freq[None, :]
    cos = jnp.cos(ang)[None, :, None, :]
    sin = jnp.sin(ang)[None, :, None, :]
    xf = x.astype(jnp.float32)
    x1, x2 = xf[..., : d // 2], xf[..., d // 2:]
    return jnp.concatenate([x1 * cos - x2 * sin, x2 * cos + x1 * sin], axis=-1).astype(x.dtype)


def block_attention(q, k, v, scale, frame_causal, log_decay_cum):
    s_len = q.shape[2]
    outs = []
    for s0 in range(0, s_len, Q_BLOCK):
        s1 = s0 + Q_BLOCK
        logits = jnp.einsum('bhqd,bhkd->bhqk', q[:, :, s0:s1], k[:, :, :s1]).astype(jnp.float32) * scale
        if log_decay_cum is not None:
            logits = logits + log_decay_cum[:, :, s0:s1, None] - log_decay_cum[:, :, None, :s1]
        q_pos = jnp.arange(s0, s1)
        k_pos = jnp.arange(s1)
        if frame_causal:
            mask = k_pos[None, :] <= q_pos[:, None]
        else:
            mask = (k_pos // CHUNK)[None, :] <= (q_pos // CHUNK)[:, None]
        p = jax.nn.softmax(jnp.where(mask, logits, -jnp.inf), axis=-1).astype(v.dtype)
        outs.append(jnp.einsum('bhqk,bhkd->bhqd', p, v[:, :, :s1]))
    return jnp.concatenate(outs, axis=2)


def fox_mixer(q, k, v, f_logit, b_f):
    b, s = q.shape[:2]
    def heads(t):
        return t.reshape(b, s, FOX_HEADS, FOX_DH).transpose(0, 2, 1, 3)
    log_f = jax.nn.log_sigmoid(f_logit.astype(jnp.float32) + b_f.astype(jnp.float32))
    c = jnp.cumsum(log_f, axis=1).transpose(0, 2, 1)
    o = block_attention(heads(q), heads(k), heads(v), FOX_DH ** -0.5, True, c)
    return o.transpose(0, 2, 1, 3).reshape(b, s, FOX_W)


def mla_mixer(c_q, c_kv, k_rope, q_norm_g, kv_norm_g, w_uq, w_ukv):
    b, s = c_q.shape[:2]
    q = (rms_norm(c_q, q_norm_g) @ w_uq).reshape(b, s, MLA_HEADS, MLA_NOPE + MLA_ROPE)
    q = jnp.concatenate([q[..., :MLA_NOPE], apply_rope(q[..., MLA_NOPE:])], axis=-1)
    kv = (rms_norm(c_kv, kv_norm_g) @ w_ukv).reshape(b, s, MLA_HEADS, MLA_NOPE + MLA_V)
    k_nope, v = kv[..., :MLA_NOPE], kv[..., MLA_NOPE:]
    k_r = apply_rope(k_rope[:, :, None, :])
    k = jnp.concatenate([k_nope, jnp.broadcast_to(k_r, (b, s, MLA_HEADS, MLA_ROPE))], axis=-1)
    o = block_attention(q.transpose(0, 2, 1, 3), k.transpose(0, 2, 1, 3), v.transpose(0, 2, 1, 3),
                        (MLA_NOPE + MLA_ROPE) ** -0.5, False, None)
    return o.transpose(0, 2, 1, 3).reshape(b, s, MLA_W)


def retention_mixer(q, k, v, g):
    b, s = q.shape[:2]
    n = s // CHUNK
    dt = v.dtype
    q = apply_rope(q.reshape(b, s, RET_HEADS, RET_DK))
    k = apply_rope(k.reshape(b, s, RET_HEADS, RET_DK)) * (RET_DK ** -0.5)
    qc = q.reshape(b, n, CHUNK, RET_HEADS, RET_DK)
    kc = k.reshape(b, n, CHUNK, RET_HEADS, RET_DK)
    vc = v.reshape(b, n, CHUNK, RET_HEADS, RET_DV)
    log_gamma = jnp.log(1.0 - 2.0 ** (-5.0 - jnp.arange(RET_HEADS, dtype=jnp.float32)))
    idx = jnp.arange(CHUNK, dtype=jnp.float32)
    intra_decay = jnp.exp(log_gamma[:, None, None] * jnp.abs(idx[:, None] - idx[None, :]))
    state_in = jnp.exp(log_gamma[:, None] * (CHUNK - 1 - idx)[None, :])
    cross_decay = jnp.exp(log_gamma[:, None] * (idx + 1.0)[None, :])
    chunk_decay = jnp.exp(log_gamma * CHUNK)
    scores = jnp.einsum('bnjhd,bnlhd->bnhjl', qc, kc) * intra_decay.astype(dt)
    intra = jnp.einsum('bnhjl,bnlhe->bnjhe', scores, vc)
    kv_chunk = jnp.einsum('bnlhd,hl,bnlhe->nbhde', kc, state_in.astype(dt), vc).astype(jnp.float32)
    def step(state, kv_n):
        return chunk_decay[None, :, None, None] * state + kv_n, state
    _, prev = lax.scan(step, jnp.zeros((b, RET_HEADS, RET_DK, RET_DV), jnp.float32), kv_chunk)
    cross = jnp.einsum('bnjhd,nbhde->bnjhe', qc, prev.astype(dt)) * cross_decay.T[None, None, :, :, None].astype(dt)
    o = (intra + cross).reshape(b, s, RET_HEADS, RET_DV).astype(jnp.float32)
    o = o * lax.rsqrt(jnp.mean(o * o, axis=-1, keepdims=True) + NORM_EPS)
    return o.reshape(b, s, RET_V_W).astype(dt) * jax.nn.silu(g)


def conv_ffn(h, w_up, w_gate, conv_w, conv_b, w_down):
    s = h.shape[1]
    u = h @ w_up
    u_pad = jnp.pad(u, ((0, 0), (CONV_W - 1, 0), (0, 0)))
    u_conv = conv_b + sum(conv_w[i] * u_pad[:, i:i + s] for i in range(CONV_W))
    return (jax.nn.gelu(u_conv) * (h @ w_gate)) @ w_down


def _normal(key, shape, fan_in):
    return jax.random.normal(key, shape, jnp.float32) * (fan_in ** -0.5)


def _gain(key, shape):
    return 1.0 + 0.02 * jax.random.normal(key, shape, jnp.float32)


def _fwd_setup_inputs(seed: int = 0) -> dict:
    key = jax.random.key(seed)
    ks = jax.random.split(key, 20)
    L, D = DEPTH, D_MODEL
    return {
        'x': jax.random.normal(ks[0], (BATCH, SEQ, D), jnp.float32),
        'norm1_g': _gain(ks[1], (L, D)),
        'w_in': _normal(ks[2], (L, D, IN_WIDTH), D),
        'mla_q_norm_g': _gain(ks[3], (L, MLA_Q_LORA)),
        'mla_kv_norm_g': _gain(ks[4], (L, MLA_KV_LORA)),
        'mla_w_uq': _normal(ks[5], (L, MLA_Q_LORA, MLA_HEADS * (MLA_NOPE + MLA_ROPE)), MLA_Q_LORA),
        'mla_w_ukv': _normal(ks[6], (L, MLA_KV_LORA, MLA_HEADS * (MLA_NOPE + MLA_V)), MLA_KV_LORA),
        'fox_b_f': FORGET_BIAS_CENTER + 0.1 * jax.random.normal(ks[7], (L, FOX_HEADS), jnp.float32),
        'w_br_fox': _normal(ks[8], (L, FOX_W, D), FOX_W),
        'w_br_mla': _normal(ks[9], (L, MLA_W, D), MLA_W),
        'w_br_ret': _normal(ks[10], (L, RET_V_W, D), RET_V_W),
        'w_out': _normal(ks[11], (L, D, D), D),
        'norm2_g': _gain(ks[12], (L, D)),
        'ffn_w_up': _normal(ks[13], (L, D, D_FF), D),
        'ffn_w_gate': _normal(ks[14], (L, D, D_FF), D),
        'ffn_conv_w': _normal(ks[15], (L, CONV_W, D_FF), CONV_W),
        'ffn_conv_b': 0.02 * jax.random.normal(ks[16], (L, D_FF), jnp.float32),
        'ffn_w_down': _normal(ks[17], (L, D_FF, D), D_FF),
        'final_norm_g': _gain(ks[18], (D,)),
    }


def _fwd_reference(x, norm1_g, w_in, mla_q_norm_g, mla_kv_norm_g, mla_w_uq, mla_w_ukv, fox_b_f,
              w_br_fox, w_br_mla, w_br_ret, w_out, norm2_g, ffn_w_up, ffn_w_gate,
              ffn_conv_w, ffn_conv_b, ffn_w_down, final_norm_g):
    b, s, d = x.shape
    split_points = [int(p) for p in np.cumsum(IN_SPLITS)[:-1]]
    for i in range(DEPTH):
        h = rms_norm(x, norm1_g[i])
        (fq, fk, fv, ff, mq, mkv, mkr, rq, rk, rv, rg, gates) = jnp.split(h @ w_in[i], split_points, axis=-1)
        a = fox_mixer(fq, fk, fv, ff, fox_b_f[i])
        bm = mla_mixer(mq, mkv, mkr, mla_q_norm_g[i], mla_kv_norm_g[i], mla_w_uq[i], mla_w_ukv[i])
        c = retention_mixer(rq, rk, rv, rg)
        g = jax.nn.sigmoid(gates.astype(jnp.float32)).astype(x.dtype).reshape(b, s, N_BRANCH, d)
        merged = (g[:, :, 0] * (a @ w_br_fox[i])
                  + g[:, :, 1] * (bm @ w_br_mla[i])
                  + g[:, :, 2] * (c @ w_br_ret[i]))
        x = x + merged @ w_out[i]
        x = x + conv_ffn(rms_norm(x, norm2_g[i]), ffn_w_up[i], ffn_w_gate[i],
                         ffn_conv_w[i], ffn_conv_b[i], ffn_w_down[i])
    return rms_norm(x, final_norm_g)


import jax as _jax
import jax.numpy as _jnp

TWIN_FORMAT = 'train_step'
FWD_PARAMS = ['x', 'norm1_g', 'w_in', 'mla_q_norm_g', 'mla_kv_norm_g', 'mla_w_uq', 'mla_w_ukv', 'fox_b_f', 'w_br_fox', 'w_br_mla', 'w_br_ret', 'w_out', 'norm2_g', 'ffn_w_up', 'ffn_w_gate', 'ffn_conv_w', 'ffn_conv_b', 'ffn_w_down', 'final_norm_g']
TWIN_WEIGHTS = ['norm1_g', 'w_in', 'mla_q_norm_g', 'mla_kv_norm_g', 'mla_w_uq', 'mla_w_ukv', 'fox_b_f', 'w_br_fox', 'w_br_mla', 'w_br_ret', 'w_out', 'norm2_g', 'ffn_w_up', 'ffn_w_gate', 'ffn_conv_w', 'ffn_conv_b', 'ffn_w_down', 'final_norm_g']
TWIN_DIFF_INPUT = 'x'
TWIN_INPUTS = ['x', 'norm1_g', 'w_in', 'mla_q_norm_g', 'mla_kv_norm_g', 'mla_w_uq', 'mla_w_ukv', 'fox_b_f', 'w_br_fox', 'w_br_mla', 'w_br_ret', 'w_out', 'norm2_g', 'ffn_w_up', 'ffn_w_gate', 'ffn_conv_w', 'ffn_conv_b', 'ffn_w_down', 'final_norm_g', 'loss_target', 'm_norm1_g', 'm_w_in', 'm_mla_q_norm_g', 'm_mla_kv_norm_g', 'm_mla_w_uq', 'm_mla_w_ukv', 'm_fox_b_f', 'm_w_br_fox', 'm_w_br_mla', 'm_w_br_ret', 'm_w_out', 'm_norm2_g', 'm_ffn_w_up', 'm_ffn_w_gate', 'm_ffn_conv_w', 'm_ffn_conv_b', 'm_ffn_w_down', 'm_final_norm_g', 'v_norm1_g', 'v_w_in', 'v_mla_q_norm_g', 'v_mla_kv_norm_g', 'v_mla_w_uq', 'v_mla_w_ukv', 'v_fox_b_f', 'v_w_br_fox', 'v_w_br_mla', 'v_w_br_ret', 'v_w_out', 'v_norm2_g', 'v_ffn_w_up', 'v_ffn_w_gate', 'v_ffn_conv_w', 'v_ffn_conv_b', 'v_ffn_w_down', 'v_final_norm_g']
TWIN_OUTPUTS = ['loss', 'grad_x', 'grad_norm1_g', 'grad_w_in', 'grad_mla_q_norm_g', 'grad_mla_kv_norm_g', 'grad_mla_w_uq', 'grad_mla_w_ukv', 'grad_fox_b_f', 'grad_w_br_fox', 'grad_w_br_mla', 'grad_w_br_ret', 'grad_w_out', 'grad_norm2_g', 'grad_ffn_w_up', 'grad_ffn_w_gate', 'grad_ffn_conv_w', 'grad_ffn_conv_b', 'grad_ffn_w_down', 'grad_final_norm_g', 'delta_norm1_g', 'delta_w_in', 'delta_mla_q_norm_g', 'delta_mla_kv_norm_g', 'delta_mla_w_uq', 'delta_mla_w_ukv', 'delta_fox_b_f', 'delta_w_br_fox', 'delta_w_br_mla', 'delta_w_br_ret', 'delta_w_out', 'delta_norm2_g', 'delta_ffn_w_up', 'delta_ffn_w_gate', 'delta_ffn_conv_w', 'delta_ffn_conv_b', 'delta_ffn_w_down', 'delta_final_norm_g', 'new_m_norm1_g', 'new_m_w_in', 'new_m_mla_q_norm_g', 'new_m_mla_kv_norm_g', 'new_m_mla_w_uq', 'new_m_mla_w_ukv', 'new_m_fox_b_f', 'new_m_w_br_fox', 'new_m_w_br_mla', 'new_m_w_br_ret', 'new_m_w_out', 'new_m_norm2_g', 'new_m_ffn_w_up', 'new_m_ffn_w_gate', 'new_m_ffn_conv_w', 'new_m_ffn_conv_b', 'new_m_ffn_w_down', 'new_m_final_norm_g', 'new_v_norm1_g', 'new_v_w_in', 'new_v_mla_q_norm_g', 'new_v_mla_kv_norm_g', 'new_v_mla_w_uq', 'new_v_mla_w_ukv', 'new_v_fox_b_f', 'new_v_w_br_fox', 'new_v_w_br_mla', 'new_v_w_br_ret', 'new_v_w_out', 'new_v_norm2_g', 'new_v_ffn_w_up', 'new_v_ffn_w_gate', 'new_v_ffn_conv_w', 'new_v_ffn_conv_b', 'new_v_ffn_w_down', 'new_v_final_norm_g']
TWIN_LEAF_KINDS = {'loss': 'loss', 'grad_x': 'grad_x', 'grad_norm1_g': 'grad_w', 'grad_w_in': 'grad_w', 'grad_mla_q_norm_g': 'grad_w', 'grad_mla_kv_norm_g': 'grad_w', 'grad_mla_w_uq': 'grad_w', 'grad_mla_w_ukv': 'grad_w', 'grad_fox_b_f': 'grad_w', 'grad_w_br_fox': 'grad_w', 'grad_w_br_mla': 'grad_w', 'grad_w_br_ret': 'grad_w', 'grad_w_out': 'grad_w', 'grad_norm2_g': 'grad_w', 'grad_ffn_w_up': 'grad_w', 'grad_ffn_w_gate': 'grad_w', 'grad_ffn_conv_w': 'grad_w', 'grad_ffn_conv_b': 'grad_w', 'grad_ffn_w_down': 'grad_w', 'grad_final_norm_g': 'grad_w', 'delta_norm1_g': 'delta_w', 'delta_w_in': 'delta_w', 'delta_mla_q_norm_g': 'delta_w', 'delta_mla_kv_norm_g': 'delta_w', 'delta_mla_w_uq': 'delta_w', 'delta_mla_w_ukv': 'delta_w', 'delta_fox_b_f': 'delta_w', 'delta_w_br_fox': 'delta_w', 'delta_w_br_mla': 'delta_w', 'delta_w_br_ret': 'delta_w', 'delta_w_out': 'delta_w', 'delta_norm2_g': 'delta_w', 'delta_ffn_w_up': 'delta_w', 'delta_ffn_w_gate': 'delta_w', 'delta_ffn_conv_w': 'delta_w', 'delta_ffn_conv_b': 'delta_w', 'delta_ffn_w_down': 'delta_w', 'delta_final_norm_g': 'delta_w', 'new_m_norm1_g': 'new_m', 'new_m_w_in': 'new_m', 'new_m_mla_q_norm_g': 'new_m', 'new_m_mla_kv_norm_g': 'new_m', 'new_m_mla_w_uq': 'new_m', 'new_m_mla_w_ukv': 'new_m', 'new_m_fox_b_f': 'new_m', 'new_m_w_br_fox': 'new_m', 'new_m_w_br_mla': 'new_m', 'new_m_w_br_ret': 'new_m', 'new_m_w_out': 'new_m', 'new_m_norm2_g': 'new_m', 'new_m_ffn_w_up': 'new_m', 'new_m_ffn_w_gate': 'new_m', 'new_m_ffn_conv_w': 'new_m', 'new_m_ffn_conv_b': 'new_m', 'new_m_ffn_w_down': 'new_m', 'new_m_final_norm_g': 'new_m', 'new_v_norm1_g': 'new_v', 'new_v_w_in': 'new_v', 'new_v_mla_q_norm_g': 'new_v', 'new_v_mla_kv_norm_g': 'new_v', 'new_v_mla_w_uq': 'new_v', 'new_v_mla_w_ukv': 'new_v', 'new_v_fox_b_f': 'new_v', 'new_v_w_br_fox': 'new_v', 'new_v_w_br_mla': 'new_v', 'new_v_w_br_ret': 'new_v', 'new_v_w_out': 'new_v', 'new_v_norm2_g': 'new_v', 'new_v_ffn_w_up': 'new_v', 'new_v_ffn_w_gate': 'new_v', 'new_v_ffn_conv_w': 'new_v', 'new_v_ffn_conv_b': 'new_v', 'new_v_ffn_w_down': 'new_v', 'new_v_final_norm_g': 'new_v'}


def _forward(args):
    return _fwd_reference(*[args[k] for k in FWD_PARAMS])


def _output_shape():
    out = _jax.eval_shape(lambda: _forward(_fwd_setup_inputs(0)))
    return out.shape, out.dtype

N_MICROBATCH = 1
ADAM_LR = 0.001
ADAM_B1 = 0.9
ADAM_B2 = 0.999
ADAM_EPS = 1e-08
ADAM_WD = 0.01
ADAM_STEP = 10
PER_EXAMPLE_BATCH_AXIS = {'x': 0, 'loss_target': 0}
SHARED_INPUTS = []
_WEIGHT_DTYPES = {'norm1_g': _jnp.float32, 'w_in': _jnp.float32, 'mla_q_norm_g': _jnp.float32, 'mla_kv_norm_g': _jnp.float32, 'mla_w_uq': _jnp.float32, 'mla_w_ukv': _jnp.float32, 'fox_b_f': _jnp.float32, 'w_br_fox': _jnp.float32, 'w_br_mla': _jnp.float32, 'w_br_ret': _jnp.float32, 'w_out': _jnp.float32, 'norm2_g': _jnp.float32, 'ffn_w_up': _jnp.float32, 'ffn_w_gate': _jnp.float32, 'ffn_conv_w': _jnp.float32, 'ffn_conv_b': _jnp.float32, 'ffn_w_down': _jnp.float32, 'final_norm_g': _jnp.float32}
MOMENT_SCALE = {'norm1_g': 7.575798e-02, 'w_in': 3.044798e-02, 'mla_q_norm_g': 1.639279e-02, 'mla_kv_norm_g': 3.403204e-02, 'mla_w_uq': 1.082635e-02, 'mla_w_ukv': 1.334547e-02, 'fox_b_f': 1.381467e-01, 'w_br_fox': 1.802061e-02, 'w_br_mla': 9.333897e-03, 'w_br_ret': 3.301108e-02, 'w_out': 3.860352e-02, 'norm2_g': 7.637844e-02, 'ffn_w_up': 3.306519e-02, 'ffn_w_gate': 3.204095e-02, 'ffn_conv_w': 3.338023e-02, 'ffn_conv_b': 3.157884e-02, 'ffn_w_down': 5.312227e-02, 'final_norm_g': 1.598052e+01}


def _to_microbatches(a, axis):
    t = _jnp.moveaxis(a, axis, 0)
    t = t.reshape((N_MICROBATCH, t.shape[0] // N_MICROBATCH) + t.shape[1:])
    return _jnp.moveaxis(t, 1, axis + 1)


def setup_inputs(seed: int = 0) -> dict:
    inp = _fwd_setup_inputs(seed)
    key = _jax.random.fold_in(_jax.random.key(seed), 7919)
    shape, _ = _output_shape()
    out = dict(inp)
    out["loss_target"] = _jax.random.normal(_jax.random.fold_in(key, 0), shape, _jnp.float32)
    for i, name in enumerate(TWIN_WEIGHTS):
        w = inp[name].astype(_jnp.float32)
        if MOMENT_SCALE is None:
            s = _jnp.sqrt(_jnp.mean(_jnp.square(w)) + 1e-30)
        else:
            s = MOMENT_SCALE[name]
        km, kv = _jax.random.split(_jax.random.fold_in(key, i + 1))
        out[name] = w
        out["m_" + name] = s * _jax.random.normal(km, w.shape, _jnp.float32)
        out["v_" + name] = (s * s) * _jax.random.uniform(kv, w.shape, _jnp.float32, 0.5, 1.5)
    if N_MICROBATCH > 1:
        for name, axis in PER_EXAMPLE_BATCH_AXIS.items():
            out[name] = _to_microbatches(out[name], axis)
    return {'x': out['x'], 'norm1_g': out['norm1_g'], 'w_in': out['w_in'], 'mla_q_norm_g': out['mla_q_norm_g'], 'mla_kv_norm_g': out['mla_kv_norm_g'], 'mla_w_uq': out['mla_w_uq'], 'mla_w_ukv': out['mla_w_ukv'], 'fox_b_f': out['fox_b_f'], 'w_br_fox': out['w_br_fox'], 'w_br_mla': out['w_br_mla'], 'w_br_ret': out['w_br_ret'], 'w_out': out['w_out'], 'norm2_g': out['norm2_g'], 'ffn_w_up': out['ffn_w_up'], 'ffn_w_gate': out['ffn_w_gate'], 'ffn_conv_w': out['ffn_conv_w'], 'ffn_conv_b': out['ffn_conv_b'], 'ffn_w_down': out['ffn_w_down'], 'final_norm_g': out['final_norm_g'], 'loss_target': out['loss_target'], 'm_norm1_g': out['m_norm1_g'], 'm_w_in': out['m_w_in'], 'm_mla_q_norm_g': out['m_mla_q_norm_g'], 'm_mla_kv_norm_g': out['m_mla_kv_norm_g'], 'm_mla_w_uq': out['m_mla_w_uq'], 'm_mla_w_ukv': out['m_mla_w_ukv'], 'm_fox_b_f': out['m_fox_b_f'], 'm_w_br_fox': out['m_w_br_fox'], 'm_w_br_mla': out['m_w_br_mla'], 'm_w_br_ret': out['m_w_br_ret'], 'm_w_out': out['m_w_out'], 'm_norm2_g': out['m_norm2_g'], 'm_ffn_w_up': out['m_ffn_w_up'], 'm_ffn_w_gate': out['m_ffn_w_gate'], 'm_ffn_conv_w': out['m_ffn_conv_w'], 'm_ffn_conv_b': out['m_ffn_conv_b'], 'm_ffn_w_down': out['m_ffn_w_down'], 'm_final_norm_g': out['m_final_norm_g'], 'v_norm1_g': out['v_norm1_g'], 'v_w_in': out['v_w_in'], 'v_mla_q_norm_g': out['v_mla_q_norm_g'], 'v_mla_kv_norm_g': out['v_mla_kv_norm_g'], 'v_mla_w_uq': out['v_mla_w_uq'], 'v_mla_w_ukv': out['v_mla_w_ukv'], 'v_fox_b_f': out['v_fox_b_f'], 'v_w_br_fox': out['v_w_br_fox'], 'v_w_br_mla': out['v_w_br_mla'], 'v_w_br_ret': out['v_w_br_ret'], 'v_w_out': out['v_w_out'], 'v_norm2_g': out['v_norm2_g'], 'v_ffn_w_up': out['v_ffn_w_up'], 'v_ffn_w_gate': out['v_ffn_w_gate'], 'v_ffn_conv_w': out['v_ffn_conv_w'], 'v_ffn_conv_b': out['v_ffn_conv_b'], 'v_ffn_w_down': out['v_ffn_w_down'], 'v_final_norm_g': out['v_final_norm_g']}


def _loss(weights, diff, rest, loss_target):
    with _jax.named_scope("forward"):
        args = {**rest, TWIN_DIFF_INPUT: diff, **{k: w.astype(_WEIGHT_DTYPES[k]) for k, w in weights.items()}}
        y = _forward(args)
    with _jax.named_scope("loss_head"):
        err = _jnp.square(y.astype(_jnp.float32) - loss_target)
        return 0.5 * _jnp.sum(_jnp.mean(err, axis=-1)) if err.ndim else 0.5 * err


def _adamw(w, g, m, v):
    m = ADAM_B1 * m + (1.0 - ADAM_B1) * g
    v = ADAM_B2 * v + (1.0 - ADAM_B2) * _jnp.square(g)
    m_hat = m / (1.0 - ADAM_B1 ** ADAM_STEP)
    v_hat = v / (1.0 - ADAM_B2 ** ADAM_STEP)
    delta = -ADAM_LR * (m_hat / (_jnp.sqrt(v_hat) + ADAM_EPS) + ADAM_WD * w)
    return delta, m, v


def reference(x, norm1_g, w_in, mla_q_norm_g, mla_kv_norm_g, mla_w_uq, mla_w_ukv, fox_b_f, w_br_fox, w_br_mla, w_br_ret, w_out, norm2_g, ffn_w_up, ffn_w_gate, ffn_conv_w, ffn_conv_b, ffn_w_down, final_norm_g, loss_target, m_norm1_g, m_w_in, m_mla_q_norm_g, m_mla_kv_norm_g, m_mla_w_uq, m_mla_w_ukv, m_fox_b_f, m_w_br_fox, m_w_br_mla, m_w_br_ret, m_w_out, m_norm2_g, m_ffn_w_up, m_ffn_w_gate, m_ffn_conv_w, m_ffn_conv_b, m_ffn_w_down, m_final_norm_g, v_norm1_g, v_w_in, v_mla_q_norm_g, v_mla_kv_norm_g, v_mla_w_uq, v_mla_w_ukv, v_fox_b_f, v_w_br_fox, v_w_br_mla, v_w_br_ret, v_w_out, v_norm2_g, v_ffn_w_up, v_ffn_w_gate, v_ffn_conv_w, v_ffn_conv_b, v_ffn_w_down, v_final_norm_g):
    given = dict(x=x, norm1_g=norm1_g, w_in=w_in, mla_q_norm_g=mla_q_norm_g, mla_kv_norm_g=mla_kv_norm_g, mla_w_uq=mla_w_uq, mla_w_ukv=mla_w_ukv, fox_b_f=fox_b_f, w_br_fox=w_br_fox, w_br_mla=w_br_mla, w_br_ret=w_br_ret, w_out=w_out, norm2_g=norm2_g, ffn_w_up=ffn_w_up, ffn_w_gate=ffn_w_gate, ffn_conv_w=ffn_conv_w, ffn_conv_b=ffn_conv_b, ffn_w_down=ffn_w_down, final_norm_g=final_norm_g, loss_target=loss_target, m_norm1_g=m_norm1_g, m_w_in=m_w_in, m_mla_q_norm_g=m_mla_q_norm_g, m_mla_kv_norm_g=m_mla_kv_norm_g, m_mla_w_uq=m_mla_w_uq, m_mla_w_ukv=m_mla_w_ukv, m_fox_b_f=m_fox_b_f, m_w_br_fox=m_w_br_fox, m_w_br_mla=m_w_br_mla, m_w_br_ret=m_w_br_ret, m_w_out=m_w_out, m_norm2_g=m_norm2_g, m_ffn_w_up=m_ffn_w_up, m_ffn_w_gate=m_ffn_w_gate, m_ffn_conv_w=m_ffn_conv_w, m_ffn_conv_b=m_ffn_conv_b, m_ffn_w_down=m_ffn_w_down, m_final_norm_g=m_final_norm_g, v_norm1_g=v_norm1_g, v_w_in=v_w_in, v_mla_q_norm_g=v_mla_q_norm_g, v_mla_kv_norm_g=v_mla_kv_norm_g, v_mla_w_uq=v_mla_w_uq, v_mla_w_ukv=v_mla_w_ukv, v_fox_b_f=v_fox_b_f, v_w_br_fox=v_w_br_fox, v_w_br_mla=v_w_br_mla, v_w_br_ret=v_w_br_ret, v_w_out=v_w_out, v_norm2_g=v_norm2_g, v_ffn_w_up=v_ffn_w_up, v_ffn_w_gate=v_ffn_w_gate, v_ffn_conv_w=v_ffn_conv_w, v_ffn_conv_b=v_ffn_conv_b, v_ffn_w_down=v_ffn_w_down, v_final_norm_g=v_final_norm_g)
    weights = {n: given[n] for n in TWIN_WEIGHTS}
    shared = {n: given[n] for n in SHARED_INPUTS}
    per_example = {n: given[n] for n in ['x']}
    grad_fn = _jax.value_and_grad(_loss, argnums=(0, 1))

    def one_microbatch(ex, loss_target):
        ex = dict(ex)
        diff = ex.pop(TWIN_DIFF_INPUT)
        return grad_fn(weights, diff, {**shared, **ex}, loss_target)

    if N_MICROBATCH == 1:
        loss, (grad_w, grad_x) = one_microbatch(per_example, given["loss_target"])
    else:
        def body(carry, xs):
            loss_sum, grad_sum = carry
            l_k, (gw_k, gx_k) = one_microbatch(xs[0], xs[1])
            with _jax.named_scope("update"):
                return (loss_sum + l_k, _jax.tree.map(_jnp.add, grad_sum, gw_k)), gx_k

        init = (_jnp.zeros((), _jnp.float32), _jax.tree.map(_jnp.zeros_like, weights))
        (loss, grad_w), grad_x = _jax.lax.scan(body, init, (per_example, given["loss_target"]))
    with _jax.named_scope("update"):
        delta_w, new_m, new_v = {}, {}, {}
        for n in TWIN_WEIGHTS:
            delta_w[n], new_m[n], new_v[n] = _adamw(weights[n], grad_w[n], given["m_" + n], given["v_" + n])
    return (loss, grad_x, *[grad_w[n] for n in TWIN_WEIGHTS], *[delta_w[n] for n in TWIN_WEIGHTS],
            *[new_m[n] for n in TWIN_WEIGHTS], *[new_v[n] for n in TWIN_WEIGHTS])
```

```python
import functools
import math

import jax
import jax.numpy as jnp
import numpy as np
from jax import lax
from jax.experimental import pallas as pl
from jax.experimental.pallas import tpu as pltpu

F32 = jnp.float32
BF16 = jnp.bfloat16
MESH = pl.DeviceIdType.MESH

CHUNK = 64
NORM_EPS = 1e-6
ROPE_THETA = 10000.0
FOX_H, FOX_DH = 6, 128
MLA_H, MLA_NOPE, MLA_ROPE, MLA_V = 6, 128, 64, 128
MLA_QL, MLA_KVL = 512, 256
MLA_DQK = 256
RET_H, RET_DK, RET_DV = 4, 128, 256
FOX_W = FOX_H * FOX_DH
MLA_W = MLA_H * MLA_V
RET_QK_W = RET_H * RET_DK
RET_V_W = RET_H * RET_DV
MLA_CAT_W = MLA_H * MLA_DQK
ADAM_LR, ADAM_B1, ADAM_B2, ADAM_EPS, ADAM_WD, ADAM_STEP = 0.001, 0.9, 0.999, 1e-08, 0.01, 10

N_CHIPS = 4
N_DEV = 8
LANES = 128
VMEM_LIMIT_BYTES = 56 * 1024 * 1024
NEG_BIG = -1e30


def _cparams(sem=None):
    return pltpu.CompilerParams(dimension_semantics=sem, vmem_limit_bytes=VMEM_LIMIT_BYTES)


def _pick(n, prefs):
    for p in prefs:
        if n % p == 0:
            return p
    return n


def _row_tile(t):
    return _pick(t, (256, 128, 64, 32, 16, 8))


def _att_tile(t):
    return 512 if t % 512 == 0 and t >= 2048 else 128


class ProjLayout:
    def __init__(self, d):
        self.d = d
        b = 3 * d
        self.g0 = 0
        self.rv = b
        self.rg = b + 1024
        self.mq = b + 2048
        self.rq = b + 2560
        self.rk = b + 3072
        self.mkv = b + 3584
        self.fq = b + 3840
        self.fk = b + 4608
        self.fv = b + 5376
        self.ff = b + 6144
        self.mkr = b + 6272
        self.width = b + 6400
        o = np.cumsum([0, FOX_W, FOX_W, FOX_W, FOX_H, MLA_QL, MLA_KVL, MLA_ROPE,
                       RET_QK_W, RET_QK_W, RET_V_W, RET_V_W, 3 * d])
        self.orig_width = int(o[-1])
        pads = [self.fq, self.fk, self.fv, self.ff, self.mq, self.mkv, self.mkr,
                self.rq, self.rk, self.rv, self.rg, self.g0]
        self.segs = [(int(o[i]), int(o[i + 1] - o[i]), pads[i]) for i in range(12)]

    def pad_cols(self, w):
        out = jnp.zeros(w.shape[:-1] + (self.width,), w.dtype)
        pieces = sorted(self.segs, key=lambda s: s[2])
        cols, pos = [], 0
        for o, n, p in pieces:
            if p > pos:
                cols.append(jnp.zeros(w.shape[:-1] + (p - pos,), w.dtype))
            cols.append(w[..., o:o + n])
            pos = p + n
        if pos < self.width:
            cols.append(jnp.zeros(w.shape[:-1] + (self.width - pos,), w.dtype))
        del out
        return jnp.concatenate(cols, axis=-1)

    def unpad_cols(self, w):
        return jnp.concatenate([w[..., p:p + n] for o, n, p in self.segs], axis=-1)


_BLK = (2048, 1792, 1536, 1408, 1024, 896, 768, 512, 384, 256, 128)
_BLK_M = (512, 256, 128)
_BLK_K = (512, 256, 128)


def mm_nn(a, b, *, name, out_dtype=BF16, res=None, ncols=None, col0=0):
    m, k = a.shape
    sharded = b.ndim == 3
    if sharded:
        s, _, ns = b.shape
        n = s * ns
        bn = _pick(ns, _BLK)
        nps = ns // bn
    else:
        n = b.shape[1] if ncols is None else ncols
        bn = _pick(n, _BLK)
    bm = _pick(m, _BLK_M)
    bk = _pick(k, _BLK_K)
    nk = k // bk
    cb0 = col0 * (n // bn)

    def body(*refs):
        if res is None:
            a_ref, b_ref, o_ref, acc_ref = refs
        else:
            a_ref, b_ref, r_ref, o_ref, acc_ref = refs
        kk = pl.program_id(2)

        @pl.when(kk == 0)
        def _():
            acc_ref[...] = jnp.zeros_like(acc_ref)

        acc_ref[...] += jnp.dot(a_ref[...].astype(BF16), b_ref[...].astype(BF16), preferred_element_type=F32)

        @pl.when(kk == nk - 1)
        def _():
            r = acc_ref[...]
            if res is not None:
                r = r + r_ref[...]
            o_ref[...] = r.astype(o_ref.dtype)

    if sharded:
        b_spec = pl.BlockSpec((None, bk, bn), lambda i, j, kk: (j // nps, kk, j % nps))
    else:
        b_spec = pl.BlockSpec((bk, bn), lambda i, j, kk: (kk, cb0 + j))
    in_specs = [pl.BlockSpec((bm, bk), lambda i, j, kk: (i, kk)), b_spec]
    args = [a, b]
    if res is not None:
        in_specs.append(pl.BlockSpec((bm, bn), lambda i, j, kk: (i, j)))
        args.append(res)
    return pl.pallas_call(
        body, name=name, grid=(m // bm, n // bn, nk),
        in_specs=in_specs, out_specs=pl.BlockSpec((bm, bn), lambda i, j, kk: (i, j)),
        out_shape=jax.ShapeDtypeStruct((m, n), out_dtype),
        scratch_shapes=[pltpu.VMEM((bm, bn), F32)],
        compiler_params=_cparams(("parallel", "parallel", "arbitrary")),
    )(*args)


def mm_nt(a, b, *, name, out_dtype=BF16, res=None):
    m, n = a.shape
    sharded = b.ndim == 3
    if sharded:
        s, k, ns = b.shape
        bn = _pick(ns, _BLK)
        nps = ns // bn
    else:
        k = b.shape[0]
        bn = _pick(n, _BLK)
    bm = _pick(m, _BLK_M)
    bko = _pick(k, _BLK)
    nn = n // bn

    def body(*refs):
        if res is None:
            a_ref, b_ref, o_ref, acc_ref = refs
        else:
            a_ref, b_ref, r_ref, o_ref, acc_ref = refs
        nidx = pl.program_id(2)

        @pl.when(nidx == 0)
        def _():
            acc_ref[...] = jnp.zeros_like(acc_ref)

        acc_ref[...] += lax.dot_general(a_ref[...].astype(BF16), b_ref[...].astype(BF16),
                                        (((1,), (1,)), ((), ())), preferred_element_type=F32)

        @pl.when(nidx == nn - 1)
        def _():
            r = acc_ref[...]
            if res is not None:
                r = r + r_ref[...]
            o_ref[...] = r.astype(o_ref.dtype)

    if sharded:
        b_spec = pl.BlockSpec((None, bko, bn), lambda i, j, q: (q // nps, j, q % nps))
    else:
        b_spec = pl.BlockSpec((bko, bn), lambda i, j, q: (j, q))
    in_specs = [pl.BlockSpec((bm, bn), lambda i, j, q: (i, q)), b_spec]
    args = [a, b]
    if res is not None:
        in_specs.append(pl.BlockSpec((bm, bko), lambda i, j, q: (i, j)))
        args.append(res)
    return pl.pallas_call(
        body, name=name, grid=(m // bm, k // bko, nn),
        in_specs=in_specs, out_specs=pl.BlockSpec((bm, bko), lambda i, j, q: (i, j)),
        out_shape=jax.ShapeDtypeStruct((m, k), out_dtype),
        scratch_shapes=[pltpu.VMEM((bm, bko), F32)],
        compiler_params=_cparams(("parallel", "parallel", "arbitrary")),
    )(*args)


def mm_tn(a, c, *, name, out_dtype=BF16, shards=1):
    m, k = a.shape
    n = c.shape[1]
    ns = n // shards
    bn = _pick(ns, _BLK)
    nps = ns // bn
    bko = _pick(k, _BLK_K)
    bm = _pick(m, (1024, 512, 256, 128))
    nm = m // bm

    def body(a_ref, c_ref, o_ref, acc_ref):
        mi = pl.program_id(2)

        @pl.when(mi == 0)
        def _():
            acc_ref[...] = jnp.zeros_like(acc_ref)

        acc_ref[...] += lax.dot_general(a_ref[...].astype(BF16), c_ref[...].astype(BF16),
                                        (((0,), (0,)), ((), ())), preferred_element_type=F32)

        @pl.when(mi == nm - 1)
        def _():
            o_ref[...] = acc_ref[...].astype(o_ref.dtype)

    if shards > 1:
        o_spec = pl.BlockSpec((None, bko, bn), lambda i, j, q: (j // nps, i, j % nps))
        o_shape = jax.ShapeDtypeStruct((shards, k, ns), out_dtype)
    else:
        o_spec = pl.BlockSpec((bko, bn), lambda i, j, q: (i, j))
        o_shape = jax.ShapeDtypeStruct((k, n), out_dtype)
    return pl.pallas_call(
        body, name=name, grid=(k // bko, n // bn, nm),
        in_specs=[pl.BlockSpec((bm, bko), lambda i, j, q: (q, i)),
                  pl.BlockSpec((bm, bn), lambda i, j, q: (q, j))],
        out_specs=o_spec, out_shape=o_shape,
        scratch_shapes=[pltpu.VMEM((bko, bn), F32)],
        compiler_params=_cparams(("parallel", "parallel", "arbitrary")),
    )(a, c)


def _rspec(tr, w, cb=0):
    return pl.BlockSpec((tr, w), lambda i: (i, cb))


def _bspec(r, w):
    return pl.BlockSpec((r, w), lambda i: (0, 0))


def _rms_rstd(x):
    return lax.rsqrt(jnp.mean(x * x, axis=-1, keepdims=True) + NORM_EPS)


def rmsnorm_fwd(x, g, *, name, width=None, cb=0):
    t = x.shape[0]
    w = x.shape[1] if width is None else width
    tr = _row_tile(t)

    def body(x_ref, g_ref, o_ref):
        xf = x_ref[...].astype(F32)
        o_ref[...] = (xf * _rms_rstd(xf) * g_ref[...]).astype(o_ref.dtype)

    return pl.pallas_call(
        body, name=name, grid=(t // tr,),
        in_specs=[_rspec(tr, w, cb), _bspec(1, w)], out_specs=_rspec(tr, w),
        out_shape=jax.ShapeDtypeStruct((t, w), BF16), compiler_params=_cparams(("parallel",)),
    )(x, g)


def rmsnorm_bwd(x, g, dy, *, name, width=None, cb=0, res=None, out_dtype=F32):
    t = x.shape[0]
    w = x.shape[1] if width is None else width
    tr = _row_tile(t)

    def body(*refs):
        if res is None:
            x_ref, g_ref, dy_ref, dx_ref, dg_ref = refs
        else:
            x_ref, g_ref, dy_ref, r_ref, dx_ref, dg_ref = refs
        xf = x_ref[...].astype(F32)
        r = _rms_rstd(xf)
        xh = xf * r
        dyf = dy_ref[...].astype(F32)
        dyg = dyf * g_ref[...]
        dx = r * (dyg - xh * jnp.mean(dyg * xh, axis=-1, keepdims=True))
        if res is not None:
            dx = dx + r_ref[...]
        dx_ref[...] = dx.astype(dx_ref.dtype)

        @pl.when(pl.program_id(0) == 0)
        def _():
            dg_ref[...] = jnp.zeros_like(dg_ref)

        dg_ref[...] += jnp.sum(dyf * xh, axis=0, keepdims=True)

    in_specs = [_rspec(tr, w, cb), _bspec(1, w), _rspec(tr, w)]
    args = [x, g, dy]
    if res is not None:
        in_specs.append(_rspec(tr, w))
        args.append(res)
    return pl.pallas_call(
        body, name=name, grid=(t // tr,), in_specs=in_specs,
        out_specs=[_rspec(tr, w), _bspec(1, w)],
        out_shape=[jax.ShapeDtypeStruct((t, w), out_dtype), jax.ShapeDtypeStruct((1, w), F32)],
        compiler_params=_cparams(("arbitrary",)),
    )(*args)


def _log_sigmoid(z):
    return jnp.minimum(z, 0.0) - jnp.log(1.0 + jnp.exp(-jnp.abs(z)))


def _split3(x):
    hi = x.astype(BF16)
    r1 = x - hi.astype(F32)
    mid = r1.astype(BF16)
    lo = (r1 - mid.astype(F32)).astype(BF16)
    return hi, mid, lo


def _tri_dot(tri, x):
    hi, mid, lo = _split3(x)
    d = functools.partial(jnp.dot, preferred_element_type=F32)
    return d(tri, lo) + d(tri, mid) + d(tri, hi)


def fox_cumsum(ff, bias, *, name):
    t = ff.shape[0]
    tr = _row_tile(t)

    def body(f_ref, b_ref, c_ref, carry_ref):
        @pl.when(pl.program_id(0) == 0)
        def _():
            carry_ref[...] = jnp.zeros_like(carry_ref)

        ls = _log_sigmoid(f_ref[...] + b_ref[...])
        row = lax.broadcasted_iota(jnp.int32, (tr, tr), 0)
        col = lax.broadcasted_iota(jnp.int32, (tr, tr), 1)
        tri = (col <= row).astype(BF16)
        c = _tri_dot(tri, ls) + carry_ref[0:1, :]
        c_ref[...] = c
        carry_ref[...] = jnp.broadcast_to(c[tr - 1:tr, :], carry_ref.shape)

    return pl.pallas_call(
        body, name=name, grid=(t // tr,), in_specs=[_rspec(tr, LANES), _bspec(1, LANES)],
        out_specs=_rspec(tr, LANES), out_shape=jax.ShapeDtypeStruct((t, LANES), F32),
        scratch_shapes=[pltpu.VMEM((8, LANES), F32)], compiler_params=_cparams(("arbitrary",)),
    )(ff, bias)


def fox_cumsum_bwd(dc, ff, bias, *, name):
    t = ff.shape[0]
    tr = _row_tile(t)
    n = t // tr

    def body(dc_ref, f_ref, b_ref, df_ref, db_ref, carry_ref):
        @pl.when(pl.program_id(0) == 0)
        def _():
            carry_ref[...] = jnp.zeros_like(carry_ref)
            db_ref[...] = jnp.zeros_like(db_ref)

        row = lax.broadcasted_iota(jnp.int32, (tr, tr), 0)
        col = lax.broadcasted_iota(jnp.int32, (tr, tr), 1)
        tri = (col >= row).astype(BF16)
        dl = _tri_dot(tri, dc_ref[...]) + carry_ref[0:1, :]
        carry_ref[...] = jnp.broadcast_to(dl[0:1, :], carry_ref.shape)
        df = dl * jax.nn.sigmoid(-(f_ref[...] + b_ref[...]))
        df_ref[...] = df.astype(df_ref.dtype)
        db_ref[...] += jnp.sum(df, axis=0, keepdims=True)

    rev = lambda i: (n - 1 - i, 0)
    return pl.pallas_call(
        body, name=name, grid=(n,),
        in_specs=[pl.BlockSpec((tr, LANES), rev), pl.BlockSpec((tr, LANES), rev), _bspec(1, LANES)],
        out_specs=[pl.BlockSpec((tr, LANES), rev), _bspec(1, LANES)],
        out_shape=[jax.ShapeDtypeStruct((t, LANES), BF16), jax.ShapeDtypeStruct((1, LANES), F32)],
        scratch_shapes=[pltpu.VMEM((8, LANES), F32)], compiler_params=_cparams(("arbitrary",)),
    )(dc, ff, bias)


def _rot(x, half):
    w = x.shape[-1]
    lane = lax.broadcasted_iota(jnp.int32, x.shape, x.ndim - 1)
    first = (lane % (2 * half)) < half
    return jnp.where(first, pltpu.roll(x, w - half, x.ndim - 1), pltpu.roll(x, half, x.ndim - 1))


def rope_tables(t, d, width, lo):
    pos = jnp.arange(t, dtype=F32)
    inv_freq = ROPE_THETA ** (-jnp.arange(0, d, 2, dtype=F32) / d)
    ang = pos[:, None] * inv_freq[None, :]
    cos, sin = jnp.cos(ang), jnp.sin(ang)
    c = jnp.ones((t, width), F32).at[:, lo:lo + d].set(jnp.concatenate([cos, cos], axis=1))
    s = jnp.zeros((t, width), F32).at[:, lo:lo + d].set(jnp.concatenate([-sin, sin], axis=1))
    return c, s


def rope_apply(x, cos, sin, *, name, half, groups, width, cb=0, scale=1.0, transpose=False, out_dtype=BF16):
    t = x.shape[0]
    tr = _row_tile(t)
    gw = cos.shape[1]
    assert gw * groups == width
    sgn = -1.0 if transpose else 1.0

    def body(x_ref, c_ref, s_ref, o_ref):
        c, s = c_ref[...], s_ref[...]
        for g in range(groups):
            xs = x_ref[:, g * gw:(g + 1) * gw].astype(F32)
            y = xs * c + sgn * _rot(xs, half) * s
            o_ref[:, g * gw:(g + 1) * gw] = (y * scale).astype(o_ref.dtype)

    return pl.pallas_call(
        body, name=name, grid=(t // tr,),
        in_specs=[_rspec(tr, width, cb), _rspec(tr, gw), _rspec(tr, gw)], out_specs=_rspec(tr, width),
        out_shape=jax.ShapeDtypeStruct((t, width), out_dtype), compiler_params=_cparams(("parallel",)),
    )(x, cos, sin)


def mla_kcat(kv, kr, *, name):
    t = kv.shape[0]
    tr = _row_tile(t)

    def body(kv_ref, kr_ref, o_ref):
        krv = kr_ref[...]
        for h in range(MLA_H):
            o_ref[:, h * MLA_DQK:h * MLA_DQK + MLA_NOPE] = kv_ref[:, h * MLA_NOPE:(h + 1) * MLA_NOPE]
            o_ref[:, h * MLA_DQK + MLA_NOPE:(h + 1) * MLA_DQK] = krv

    return pl.pallas_call(
        body, name=name, grid=(t // tr,),
        in_specs=[_rspec(tr, MLA_H * MLA_NOPE, 0), _rspec(tr, LANES)], out_specs=_rspec(tr, MLA_CAT_W),
        out_shape=jax.ShapeDtypeStruct((t, MLA_CAT_W), BF16), compiler_params=_cparams(("parallel",)),
    )(kv, kr)


def mla_kcat_bwd(dk_cat, dv, *, name):
    t = dk_cat.shape[0]
    tr = _row_tile(t)

    def body(dk_ref, dv_ref, dkv_ref, dkr_ref):
        acc = jnp.zeros((tr, LANES), F32)
        for h in range(MLA_H):
            dkv_ref[:, h * MLA_NOPE:(h + 1) * MLA_NOPE] = dk_ref[:, h * MLA_DQK:h * MLA_DQK + MLA_NOPE].astype(BF16)
            acc = acc + dk_ref[:, h * MLA_DQK + MLA_NOPE:(h + 1) * MLA_DQK].astype(F32)
        dkv_ref[:, MLA_H * MLA_NOPE:] = dv_ref[...].astype(BF16)
        dkr_ref[...] = acc

    return pl.pallas_call(
        body, name=name, grid=(t // tr,),
        in_specs=[_rspec(tr, MLA_CAT_W), _rspec(tr, MLA_W)],
        out_specs=[_rspec(tr, MLA_CAT_W), _rspec(tr, LANES)],
        out_shape=[jax.ShapeDtypeStruct((t, MLA_CAT_W), BF16), jax.ShapeDtypeStruct((t, LANES), F32)],
        compiler_params=_cparams(("parallel",)),
    )(dk_cat, dv)


def _silu(g):
    return g * jax.nn.sigmoid(g)


def ret_gate(on, proj, rg_cb, *, name):
    t = on.shape[0]
    tr = _row_tile(t)

    def body(on_ref, g_ref, o_ref):
        o_ref[...] = (on_ref[...].astype(F32) * _silu(g_ref[...].astype(F32))).astype(o_ref.dtype)

    return pl.pallas_call(
        body, name=name, grid=(t // tr,), in_specs=[_rspec(tr, RET_V_W), _rspec(tr, RET_V_W, rg_cb)],
        out_specs=_rspec(tr, RET_V_W), out_shape=jax.ShapeDtypeStruct((t, RET_V_W), BF16),
        compiler_params=_cparams(("parallel",)),
    )(on, proj)


def ret_gate_bwd(dy, on, proj, rg_cb, rstd, *, name):
    t = on.shape[0]
    tr = _row_tile(t)

    def body(dy_ref, on_ref, g_ref, r_ref, dg_ref, do_ref):
        dyf = dy_ref[...].astype(F32)
        onf = on_ref[...].astype(F32)
        g = g_ref[...].astype(F32)
        sg = jax.nn.sigmoid(g)
        dg_ref[...] = (dyf * onf * sg * (1.0 + g * (1.0 - sg))).astype(dg_ref.dtype)
        don = dyf * g * sg
        for h in range(RET_H):
            sl = slice(h * RET_DV, (h + 1) * RET_DV)
            d_h, o_h = don[:, sl], onf[:, sl]
            do_ref[:, sl] = (r_ref[h] * (d_h - o_h * jnp.mean(d_h * o_h, axis=-1, keepdims=True))).astype(do_ref.dtype)

    return pl.pallas_call(
        body, name=name, grid=(t // tr,),
        in_specs=[_rspec(tr, RET_V_W), _rspec(tr, RET_V_W), _rspec(tr, RET_V_W, rg_cb),
                  pl.BlockSpec((RET_H, tr, 1), lambda i: (0, i, 0))],
        out_specs=[_rspec(tr, RET_V_W), _rspec(tr, RET_V_W)],
        out_shape=[jax.ShapeDtypeStruct((t, RET_V_W), BF16), jax.ShapeDtypeStruct((t, RET_V_W), BF16)],
        compiler_params=_cparams(("parallel",)),
    )(dy, on, proj, rstd)


def merge_fwd(m0, m1, m2, proj, d, *, name):
    t = m0.shape[0]
    tr = _row_tile(t)

    def body(m0_ref, m1_ref, m2_ref, g0_ref, g1_ref, g2_ref, o_ref):
        acc = jax.nn.sigmoid(g0_ref[...].astype(F32)) * m0_ref[...].astype(F32)
        acc = acc + jax.nn.sigmoid(g1_ref[...].astype(F32)) * m1_ref[...].astype(F32)
        acc = acc + jax.nn.sigmoid(g2_ref[...].astype(F32)) * m2_ref[...].astype(F32)
        o_ref[...] = acc.astype(o_ref.dtype)

    return pl.pallas_call(
        body, name=name, grid=(t // tr,),
        in_specs=[_rspec(tr, d)] * 3 + [_rspec(tr, d, 0), _rspec(tr, d, 1), _rspec(tr, d, 2)],
        out_specs=_rspec(tr, d), out_shape=jax.ShapeDtypeStruct((t, d), BF16),
        compiler_params=_cparams(("parallel",)),
    )(m0, m1, m2, proj, proj, proj)


def merge_bwd(dm, m0, m1, m2, proj, d, *, name):
    t = m0.shape[0]
    tr = _row_tile(t)

    def body(dm_ref, m0_ref, m1_ref, m2_ref, g0_ref, g1_ref, g2_ref, d0_ref, d1_ref, d2_ref, dg_ref):
        dmf = dm_ref[...].astype(F32)
        for i, (m_ref, g_ref, d_ref) in enumerate(((m0_ref, g0_ref, d0_ref), (m1_ref, g1_ref, d1_ref),
                                                   (m2_ref, g2_ref, d2_ref))):
            sg = jax.nn.sigmoid(g_ref[...].astype(F32))
            d_ref[...] = (dmf * sg).astype(d_ref.dtype)
            dg_ref[:, i * d:(i + 1) * d] = (dmf * m_ref[...].astype(F32) * sg * (1.0 - sg)).astype(dg_ref.dtype)

    return pl.pallas_call(
        body, name=name, grid=(t // tr,),
        in_specs=[_rspec(tr, d)] * 4 + [_rspec(tr, d, 0), _rspec(tr, d, 1), _rspec(tr, d, 2)],
        out_specs=[_rspec(tr, d)] * 3 + [_rspec(tr, 3 * d)],
        out_shape=[jax.ShapeDtypeStruct((t, d), BF16)] * 3 + [jax.ShapeDtypeStruct((t, 3 * d), BF16)],
        compiler_params=_cparams(("parallel",)),
    )(dm, m0, m1, m2, proj, proj, proj)


_GELU_K = math.sqrt(2.0 / math.pi)
_GELU_C = 0.044715


def _gelu(x):
    return 0.5 * x * (1.0 + jnp.tanh(_GELU_K * (x + _GELU_C * x * x * x)))


def _gelu_grad(x):
    th = jnp.tanh(_GELU_K * (x + _GELU_C * x * x * x))
    return 0.5 * (1.0 + th) + 0.5 * x * (1.0 - th * th) * _GELU_K * (1.0 + 3.0 * _GELU_C * x * x)


def _shift_down(u, halo, k, first):
    tr = u.shape[0]
    r = pltpu.roll(u, k, 0)
    row = lax.broadcasted_iota(jnp.int32, u.shape, 0)
    for q in range(k):
        h = jnp.where(first, 0.0, halo[8 - k + q:8 - k + q + 1, :])
        r = jnp.where(row == q, h, r)
    del tr
    return r


def _shift_up(u, halo, k, last):
    tr = u.shape[0]
    r = pltpu.roll(u, tr - k, 0)
    row = lax.broadcasted_iota(jnp.int32, u.shape, 0)
    for q in range(k):
        h = jnp.where(last, 0.0, halo[q:q + 1, :])
        r = jnp.where(row == tr - k + q, h, r)
    return r


def _conv3(u, halo, w_ref, b_ref, first):
    return (b_ref[...] + w_ref[0:1, :] * _shift_down(u, halo, 2, first)
            + w_ref[1:2, :] * _shift_down(u, halo, 1, first) + w_ref[2:3, :] * u)


def _ffn_tiles(t, f):
    tr = _row_tile(t)
    cw = f // 4 if (f // 4) % LANES == 0 else f
    return tr, cw


def ffn_act(u, gt, conv_w, conv_b, *, name):
    t, f = u.shape
    tr, cw = _ffn_tiles(t, f)
    hb = tr // 8

    def body(u_ref, h_ref, g_ref, w_ref, b_ref, o_ref):
        first = pl.program_id(1) == 0
        uc = _conv3(u_ref[...].astype(F32), h_ref[...].astype(F32), w_ref, b_ref, first)
        o_ref[...] = (_gelu(uc) * g_ref[...].astype(F32)).astype(o_ref.dtype)

    return pl.pallas_call(
        body, name=name, grid=(f // cw, t // tr),
        in_specs=[pl.BlockSpec((tr, cw), lambda j, i: (i, j)),
                  pl.BlockSpec((8, cw), lambda j, i: (jnp.maximum(i * hb - 1, 0), j)),
                  pl.BlockSpec((tr, cw), lambda j, i: (i, j)),
                  pl.BlockSpec((3, cw), lambda j, i: (0, j)), pl.BlockSpec((1, cw), lambda j, i: (0, j))],
        out_specs=pl.BlockSpec((tr, cw), lambda j, i: (i, j)),
        out_shape=jax.ShapeDtypeStruct((t, f), BF16), compiler_params=_cparams(("parallel", "parallel")),
    )(u, u, gt, conv_w, conv_b)


def ffn_act_bwd(dact, u, gt, conv_w, conv_b, *, name):
    t, f = u.shape
    tr, cw = _ffn_tiles(t, f)
    hb = tr // 8

    def body(da_ref, u_ref, h_ref, g_ref, w_ref, b_ref, duc_ref, dg_ref, dw_ref, db_ref):
        first = pl.program_id(1) == 0

        @pl.when(first)
        def _():
            dw_ref[...] = jnp.zeros_like(dw_ref)
            db_ref[...] = jnp.zeros_like(db_ref)

        uf, hf = u_ref[...].astype(F32), h_ref[...].astype(F32)
        uc = _conv3(uf, hf, w_ref, b_ref, first)
        da = da_ref[...].astype(F32)
        dg_ref[...] = (da * _gelu(uc)).astype(dg_ref.dtype)
        duc = da * g_ref[...].astype(F32) * _gelu_grad(uc)
        duc_ref[...] = duc.astype(duc_ref.dtype)
        db_ref[...] += jnp.sum(duc, axis=0, keepdims=True)
        dw_ref[0:1, :] += jnp.sum(duc * _shift_down(uf, hf, 2, first), axis=0, keepdims=True)
        dw_ref[1:2, :] += jnp.sum(duc * _shift_down(uf, hf, 1, first), axis=0, keepdims=True)
        dw_ref[2:3, :] += jnp.sum(duc * uf, axis=0, keepdims=True)

    tile = pl.BlockSpec((tr, cw), lambda j, i: (i, j))
    return pl.pallas_call(
        body, name=name, grid=(f // cw, t // tr),
        in_specs=[tile, tile, pl.BlockSpec((8, cw), lambda j, i: (jnp.maximum(i * hb - 1, 0), j)), tile,
                  pl.BlockSpec((3, cw), lambda j, i: (0, j)), pl.BlockSpec((1, cw), lambda j, i: (0, j))],
        out_specs=[tile, tile, pl.BlockSpec((8, cw), lambda j, i: (0, j)), pl.BlockSpec((1, cw), lambda j, i: (0, j))],
        out_shape=[jax.ShapeDtypeStruct((t, f), BF16), jax.ShapeDtypeStruct((t, f), BF16),
                   jax.ShapeDtypeStruct((8, f), F32), jax.ShapeDtypeStruct((1, f), F32)],
        compiler_params=_cparams(("parallel", "arbitrary")),
    )(dact, u, u, gt, conv_w, conv_b)


def conv3_transpose(duc, conv_w, *, name):
    t, f = duc.shape
    tr, cw = _ffn_tiles(t, f)
    hb = tr // 8
    nt = t // tr

    def body(d_ref, h_ref, w_ref, o_ref):
        last = pl.program_id(1) == nt - 1
        d, h = d_ref[...].astype(F32), h_ref[...].astype(F32)
        o = w_ref[2:3, :] * d + w_ref[1:2, :] * _shift_up(d, h, 1, last) + w_ref[0:1, :] * _shift_up(d, h, 2, last)
        o_ref[...] = o.astype(o_ref.dtype)

    tile = pl.BlockSpec((tr, cw), lambda j, i: (i, j))
    return pl.pallas_call(
        body, name=name, grid=(f // cw, nt),
        in_specs=[tile, pl.BlockSpec((8, cw), lambda j, i: (jnp.minimum((i + 1) * hb, t // 8 - 1), j)),
                  pl.BlockSpec((3, cw), lambda j, i: (0, j))],
        out_specs=tile, out_shape=jax.ShapeDtypeStruct((t, f), BF16),
        compiler_params=_cparams(("parallel", "parallel")),
    )(duc, duc, conv_w)


def loss_head(x, g, target, *, name):
    t, d = x.shape
    tr = _row_tile(t)

    def body(x_ref, g_ref, t_ref, l_ref, dx_ref, dg_ref):
        @pl.when(pl.program_id(0) == 0)
        def _():
            l_ref[...] = jnp.zeros_like(l_ref)
            dg_ref[...] = jnp.zeros_like(dg_ref)

        xf = x_ref[...]
        r = _rms_rstd(xf)
        xh = xf * r
        err = xh * g_ref[...] - t_ref[...]
        l_ref[...] += 0.5 * jnp.sum(jnp.mean(err * err, axis=-1, keepdims=True), axis=0, keepdims=True)
        dy = err / d
        dyg = dy * g_ref[...]
        dx_ref[...] = r * (dyg - xh * jnp.mean(dyg * xh, axis=-1, keepdims=True))
        dg_ref[...] += jnp.sum(dy * xh, axis=0, keepdims=True)

    return pl.pallas_call(
        body, name=name, grid=(t // tr,),
        in_specs=[_rspec(tr, d), _bspec(1, d), _rspec(tr, d)],
        out_specs=[_bspec(1, LANES), _rspec(tr, d), _bspec(1, d)],
        out_shape=[jax.ShapeDtypeStruct((1, LANES), F32), jax.ShapeDtypeStruct((t, d), F32),
                   jax.ShapeDtypeStruct((1, d), F32)],
        compiler_params=_cparams(("arbitrary",)),
    )(x, g, target)


def _att_mask(mode, i, j, tq, tk):
    t_pos = i * tq + lax.broadcasted_iota(jnp.int32, (tq, tk), 0)
    s_pos = j * tk + lax.broadcasted_iota(jnp.int32, (tq, tk), 1)
    if mode == "fox":
        return s_pos <= t_pos, t_pos, s_pos
    return (s_pos // CHUNK) <= (t_pos // CHUNK), t_pos, s_pos


def _nt(a, b):
    return lax.dot_general(a, b, (((1,), (1,)), ((), ())), preferred_element_type=F32)


def _tn(a, b):
    return lax.dot_general(a, b, (((0,), (0,)), ((), ())), preferred_element_type=F32)


def _scores(mode, scale, q, k, extra, i, j, tq, tk):
    s = _nt(q, k)
    mask, t_pos, s_pos = _att_mask(mode, i, j, tq, tk)
    if mode == "fox":
        cq_ref, ck_ref = extra
        s = s * scale + cq_ref[...] - ck_ref[...]
        return jnp.where(mask, s, NEG_BIG), mask, None
    if mode == "mla":
        return jnp.where(mask, s * scale, NEG_BIG), mask, None
    (lg_ref,) = extra
    dec = jnp.exp(lg_ref[...] * jnp.abs(t_pos - s_pos).astype(F32))
    dec = jnp.where(mask, dec, 0.0)
    return s * dec, mask, dec


def attention_fwd(mode, q, k, v, *, name, h_n, dk, dv, qcb, kcb, vcb, scale=1.0, extra=()):
    t = q.shape[0]
    tq = tk = _att_tile(t)
    nq = t // tq
    softmax = mode != "ret"

    def body(*refs):
        q_ref, k_ref, v_ref = refs[:3]
        n_ex = len(extra)
        ex = refs[3:3 + n_ex]
        o_ref, st_ref, m_ref, l_ref, acc_ref = refs[3 + n_ex:]
        i, j = pl.program_id(1), pl.program_id(2)

        @pl.when(j == 0)
        def _():
            m_ref[...] = jnp.full_like(m_ref, NEG_BIG)
            l_ref[...] = jnp.zeros_like(l_ref)
            acc_ref[...] = jnp.zeros_like(acc_ref)

        @pl.when(j <= i)
        def _():
            s, _, _ = _scores(mode, scale, q_ref[...], k_ref[...], ex, i, j, tq, tk)
            if softmax:
                m_old = m_ref[...]
                m_new = jnp.maximum(m_old, jnp.max(s, axis=-1, keepdims=True))
                p = jnp.exp(s - m_new)
                alpha = jnp.exp(m_old - m_new)
                l_ref[...] = alpha * l_ref[...] + jnp.sum(p, axis=-1, keepdims=True)
                acc_ref[...] = alpha * acc_ref[...] + jnp.dot(p.astype(BF16), v_ref[...], preferred_element_type=F32)
                m_ref[...] = m_new
            else:
                acc_ref[...] += jnp.dot(s.astype(BF16), v_ref[...], preferred_element_type=F32)

        @pl.when(j == i)
        def _():
            if softmax:
                o_ref[...] = (acc_ref[...] / l_ref[...]).astype(o_ref.dtype)
                st_ref[...] = m_ref[...] + jnp.log(l_ref[...])
            else:
                o = acc_ref[...]
                r = lax.rsqrt(jnp.mean(o * o, axis=-1, keepdims=True) + NORM_EPS)
                o_ref[...] = (o * r).astype(o_ref.dtype)
                st_ref[...] = r

    in_specs = [pl.BlockSpec((tq, dk), lambda h, i, j: (i, qcb + h)),
                pl.BlockSpec((tk, dk), lambda h, i, j: (jnp.minimum(j, i), kcb + h)),
                pl.BlockSpec((tk, dv), lambda h, i, j: (jnp.minimum(j, i), vcb + h))]
    if mode == "fox":
        in_specs += [pl.BlockSpec((None, tq, 1), lambda h, i, j: (h, i, 0)),
                     pl.BlockSpec((None, 1, tk), lambda h, i, j: (h, 0, jnp.minimum(j, i)))]
    elif mode == "ret":
        in_specs += [pl.BlockSpec((None, 1, 1), lambda h, i, j: (h, 0, 0))]
    return pl.pallas_call(
        body, name=name, grid=(h_n, nq, nq), in_specs=in_specs,
        out_specs=[pl.BlockSpec((tq, dv), lambda h, i, j: (i, h)),
                   pl.BlockSpec((None, tq, 1), lambda h, i, j: (h, i, 0))],
        out_shape=[jax.ShapeDtypeStruct((t, h_n * dv), BF16), jax.ShapeDtypeStruct((h_n, t, 1), F32)],
        scratch_shapes=[pltpu.VMEM((tq, 1), F32), pltpu.VMEM((tq, 1), F32), pltpu.VMEM((tq, dv), F32)],
        compiler_params=_cparams(("parallel", "parallel", "arbitrary")),
    )(q, k, v, *extra)


def attention_bwd(mode, q, k, v, o, do, stat, *, name, h_n, dk, dv, qcb, kcb, vcb, scale=1.0, extra=()):
    t = q.shape[0]
    tq = tk = _att_tile(t)
    nq = t // tq
    softmax = mode != "ret"

    def body(*refs):
        q_ref, k_ref, v_ref, o_ref, do_ref, st_ref = refs[:6]
        n_ex = len(extra)
        ex = refs[6:6 + n_ex]
        outs = refs[6 + n_ex:]
        if mode == "fox":
            dq_ref, dk_ref, dv_ref, dck_ref, dcq_ref, dk_acc, dv_acc, dck_acc = outs
        else:
            dq_ref, dk_ref, dv_ref, dk_acc, dv_acc = outs
        j, i = pl.program_id(1), pl.program_id(2)

        @pl.when((j == 0) & (i == 0))
        def _():
            dq_ref[...] = jnp.zeros_like(dq_ref)
            if mode == "fox":
                dcq_ref[...] = jnp.zeros_like(dcq_ref)

        @pl.when(i == j)
        def _():
            dk_acc[...] = jnp.zeros_like(dk_acc)
            dv_acc[...] = jnp.zeros_like(dv_acc)
            if mode == "fox":
                dck_acc[...] = jnp.zeros_like(dck_acc)

        @pl.when(i >= j)
        def _():
            qv, kv_, vv, dov = q_ref[...], k_ref[...], v_ref[...], do_ref[...]
            s, mask, dec = _scores(mode, scale, qv, kv_, ex, i, j, tq, tk)
            dp = _nt(dov, vv)
            if softmax:
                p = jnp.where(mask, jnp.exp(s - st_ref[...]), 0.0)
                delta = jnp.sum(dov.astype(F32) * o_ref[...].astype(F32), axis=-1, keepdims=True)
                ds = p * (dp - delta)
            else:
                p = s
                ds = dp * dec
            dv_acc[...] += _tn(p.astype(BF16), dov)
            dsb = ds.astype(BF16)
            dk_acc[...] += _tn(dsb, qv)
            rows = pl.ds(pl.multiple_of(i * tq, tq), tq)
            dq_ref[rows, :] += jnp.dot(dsb, kv_, preferred_element_type=F32) * scale
            if mode == "fox":
                dck_acc[...] += jnp.sum(ds, axis=0, keepdims=True)
                dcq_ref[rows, :] += jnp.sum(ds, axis=1, keepdims=True)

        @pl.when(i == nq - 1)
        def _():
            dk_ref[...] = (dk_acc[...] * scale).astype(dk_ref.dtype)
            dv_ref[...] = dv_acc[...].astype(dv_ref.dtype)
            if mode == "fox":
                dck_ref[...] = dck_acc[...]

    qi = lambda h, j, i: (jnp.maximum(i, j), qcb + h)
    in_specs = [pl.BlockSpec((tq, dk), qi),
                pl.BlockSpec((tk, dk), lambda h, j, i: (j, kcb + h)),
                pl.BlockSpec((tk, dv), lambda h, j, i: (j, vcb + h)),
                pl.BlockSpec((tq, dv), lambda h, j, i: (jnp.maximum(i, j), h)),
                pl.BlockSpec((tq, dv), lambda h, j, i: (jnp.maximum(i, j), h)),
                pl.BlockSpec((None, tq, 1), lambda h, j, i: (h, jnp.maximum(i, j), 0))]
    out_specs = [pl.BlockSpec((t, dk), lambda h, j, i: (0, h)),
                 pl.BlockSpec((tk, dk), lambda h, j, i: (j, h)),
                 pl.BlockSpec((tk, dv), lambda h, j, i: (j, h))]
    out_shape = [jax.ShapeDtypeStruct((t, h_n * dk), F32), jax.ShapeDtypeStruct((t, h_n * dk), BF16),
                 jax.ShapeDtypeStruct((t, h_n * dv), BF16)]
    scratch = [pltpu.VMEM((tk, dk), F32), pltpu.VMEM((tk, dv), F32)]
    if mode == "fox":
        in_specs += [pl.BlockSpec((None, tq, 1), lambda h, j, i: (h, jnp.maximum(i, j), 0)),
                     pl.BlockSpec((None, 1, tk), lambda h, j, i: (h, 0, j))]
        out_specs += [pl.BlockSpec((None, 1, tk), lambda h, j, i: (h, 0, j)),
                      pl.BlockSpec((None, t, 1), lambda h, j, i: (h, 0, 0))]
        out_shape += [jax.ShapeDtypeStruct((h_n, 1, t), F32), jax.ShapeDtypeStruct((h_n, t, 1), F32)]
        scratch.append(pltpu.VMEM((1, tk), F32))
    elif mode == "ret":
        in_specs += [pl.BlockSpec((None, 1, 1), lambda h, j, i: (h, 0, 0))]
    return pl.pallas_call(
        body, name=name, grid=(h_n, nq, nq), in_specs=in_specs, out_specs=out_specs, out_shape=out_shape,
        scratch_shapes=scratch, compiler_params=_cparams(("parallel", "arbitrary", "arbitrary")),
    )(q, k, v, o, do, stat, *extra)


_HBM = pl.BlockSpec(memory_space=pltpu.HBM)
_VMEM = pl.BlockSpec(memory_space=pltpu.VMEM)


def _place():
    x, y, c = lax.axis_index("x"), lax.axis_index("y"), lax.axis_index("c")
    chips = [(1 - x, y), (x, 1 - y), (1 - x, 1 - y)]
    return x, y, c, chips


def allgather_weights(shards, *, name):
    na = len(shards)

    def body(*refs):
        x_refs, out_refs = refs[:na], refs[na:2 * na]
        send_sems, recv_sems, local_sems = refs[2 * na:]
        x, y, c, chips = _place()
        my = 2 * x + y
        me, sibling = (x, y, c), (x, y, 1 - c)

        def blk(a, chip, half):
            rh = shards[a].shape[0] // 2
            return out_refs[a].at[chip, pl.ds(half * rh, rh), :]

        def copy(a, k, src, dst, to):
            return pltpu.make_async_remote_copy(src_ref=src, dst_ref=dst, send_sem=send_sems.at[6 * a + k],
                                                recv_sem=recv_sems.at[6 * a + k], device_id=to, device_id_type=MESH)

        mine, first, passed = [], [], []
        for a in range(na):
            rh = shards[a].shape[0] // 2
            cp = pltpu.make_async_copy(x_refs[a], out_refs[a].at[my], local_sems.at[a])
            cp.start()
            mine.append(cp)
            for j, chip in enumerate(chips):
                cp = copy(a, j, x_refs[a].at[pl.ds(c * rh, rh), :], blk(a, my, c), (*chip, c))
                cp.start()
                first.append(cp)
        for a in range(na):
            for j, (px, py) in enumerate(chips):
                got = blk(a, 2 * px + py, c)
                copy(a, j, got, got, me).wait_recv()
                cp = copy(a, 3 + j, got, got, sibling)
                cp.start()
                passed.append(cp)
        for a in range(na):
            for j, (px, py) in enumerate(chips):
                got = blk(a, 2 * px + py, 1 - c)
                copy(a, 3 + j, got, got, me).wait_recv()
        for cp in first + passed:
            cp.wait_send()
        for cp in mine:
            cp.wait()

    return pl.pallas_call(
        body, name=name, in_specs=[_HBM] * na, out_specs=[_HBM] * na,
        out_shape=[jax.ShapeDtypeStruct((N_CHIPS,) + s.shape, s.dtype) for s in shards],
        scratch_shapes=[pltpu.SemaphoreType.DMA((6 * na,)), pltpu.SemaphoreType.DMA((6 * na,)),
                        pltpu.SemaphoreType.DMA((na,))],
    )(*shards)


def sibling_swap_halves(gs, *, name):
    na = len(gs)

    def body(*refs):
        g_refs, out_refs = refs[:na], refs[na:2 * na]
        send_sems, recv_sems = refs[2 * na:]
        x, y, c, _ = _place()
        cps = []
        for a in range(na):
            rh = gs[a].shape[1] // 2
            cp = pltpu.make_async_remote_copy(
                src_ref=g_refs[a].at[:, pl.ds((1 - c) * rh, rh), :], dst_ref=out_refs[a], send_sem=send_sems.at[a],
                recv_sem=recv_sems.at[a], device_id=(x, y, 1 - c), device_id_type=MESH)
            cp.start()
            cps.append(cp)
        for cp in cps:
            cp.wait()

    return pl.pallas_call(
        body, name=name, in_specs=[_HBM] * na, out_specs=[_HBM] * na,
        out_shape=[jax.ShapeDtypeStruct((g.shape[0], g.shape[1] // 2, g.shape[2]), g.dtype) for g in gs],
        scratch_shapes=[pltpu.SemaphoreType.DMA((na,)), pltpu.SemaphoreType.DMA((na,))],
    )(*gs)


def chip_scatter(ps, *, name):
    na = len(ps)

    def body(*refs):
        p_refs, out_refs = refs[:na], refs[na:2 * na]
        send_sems, recv_sems, local_sems = refs[2 * na:]
        x, y, c, chips = _place()
        my = 2 * x + y
        mine, sends = [], []
        for a in range(na):
            cp = pltpu.make_async_copy(p_refs[a].at[my], out_refs[a].at[my], local_sems.at[a])
            cp.start()
            mine.append(cp)
            for j, (px, py) in enumerate(chips):
                cp = pltpu.make_async_remote_copy(
                    src_ref=p_refs[a].at[2 * px + py], dst_ref=out_refs[a].at[my], send_sem=send_sems.at[3 * a + j],
                    recv_sem=recv_sems.at[3 * a + j], device_id=(px, py, c), device_id_type=MESH)
                cp.start()
                sends.append(cp)
        for a in range(na):
            for j, (px, py) in enumerate(chips):
                got = out_refs[a].at[2 * px + py]
                pltpu.make_async_remote_copy(src_ref=got, dst_ref=got, send_sem=send_sems.at[3 * a + j],
                                             recv_sem=recv_sems.at[3 * a + j], device_id=(x, y, c),
                                             device_id_type=MESH).wait_recv()
        for cp in sends:
            cp.wait_send()
        for cp in mine:
            cp.wait()

    return pl.pallas_call(
        body, name=name, in_specs=[_HBM] * na, out_specs=[_HBM] * na,
        out_shape=[jax.ShapeDtypeStruct(p.shape, p.dtype) for p in ps],
        scratch_shapes=[pltpu.SemaphoreType.DMA((3 * na,)), pltpu.SemaphoreType.DMA((3 * na,)),
                        pltpu.SemaphoreType.DMA((na,))],
    )(*ps)


def sibling_share(reds, *, name):
    na = len(reds)

    def body(*refs):
        r_refs, out_refs = refs[:na], refs[na:2 * na]
        send_sems, recv_sems, local_sems = refs[2 * na:]
        x, y, c, _ = _place()
        mine, sends = [], []
        for a in range(na):
            cp = pltpu.make_async_copy(r_refs[a], out_refs[a].at[c], local_sems.at[a])
            cp.start()
            mine.append(cp)
            cp = pltpu.make_async_remote_copy(
                src_ref=r_refs[a], dst_ref=out_refs[a].at[c], send_sem=send_sems.at[a], recv_sem=recv_sems.at[a],
                device_id=(x, y, 1 - c), device_id_type=MESH)
            cp.start()
            sends.append(cp)
        for a in range(na):
            got = out_refs[a].at[1 - c]
            pltpu.make_async_remote_copy(src_ref=got, dst_ref=got, send_sem=send_sems.at[a], recv_sem=recv_sems.at[a],
                                         device_id=(x, y, c), device_id_type=MESH).wait_recv()
        for cp in sends:
            cp.wait_send()
        for cp in mine:
            cp.wait()

    return pl.pallas_call(
        body, name=name, in_specs=[_HBM] * na, out_specs=[_HBM] * na,
        out_shape=[jax.ShapeDtypeStruct((2,) + r.shape, r.dtype) for r in reds],
        scratch_shapes=[pltpu.SemaphoreType.DMA((na,)), pltpu.SemaphoreType.DMA((na,)),
                        pltpu.SemaphoreType.DMA((na,))],
    )(*reds)


def allgather_small(v, *, name):
    r, cols = v.shape

    def body(v_ref, out_ref, send_sems, recv_sems):
        x, y, c, _ = _place()
        me = 4 * x + 2 * y + c
        out_ref[me] = v_ref[...]
        peers = []
        for m in range(1, N_DEV):
            px = 1 - x if m & 4 else x
            py = 1 - y if m & 2 else y
            pc = 1 - c if m & 1 else c
            peers.append((px, py, pc))
        sends = []
        for k, peer in enumerate(peers):
            cp = pltpu.make_async_remote_copy(src_ref=v_ref, dst_ref=out_ref.at[me], send_sem=send_sems.at[k],
                                              recv_sem=recv_sems.at[k], device_id=peer, device_id_type=MESH)
            cp.start()
            sends.append(cp)
        for k, (px, py, pc) in enumerate(peers):
            got = out_ref.at[4 * px + 2 * py + pc]
            pltpu.make_async_remote_copy(src_ref=got, dst_ref=got, send_sem=send_sems.at[k], recv_sem=recv_sems.at[k],
                                         device_id=(x, y, c), device_id_type=MESH).wait_recv()
        for cp in sends:
            cp.wait_send()

    return pl.pallas_call(
        body, name=name, in_specs=[_VMEM], out_specs=_VMEM,
        out_shape=jax.ShapeDtypeStruct((N_DEV, r, cols), v.dtype),
        scratch_shapes=[pltpu.SemaphoreType.DMA((N_DEV - 1,)), pltpu.SemaphoreType.DMA((N_DEV - 1,))],
        compiler_params=pltpu.CompilerParams(vmem_limit_bytes=VMEM_LIMIT_BYTES),
    )(v)


def _slot_rows(r, cols, n_blocks):
    budget = max(16, (16 * 1024 * 1024) // (2 * n_blocks * 2 * cols))
    for cand in (1024, 512, 256, 128, 64, 32, 16):
        if cand <= budget and r % cand == 0:
            return cand
    return r


def add_slots(a, b, *, name):
    s, r, cols = a.shape
    tr = _slot_rows(r, cols, 3)

    def body(a_ref, b_ref, o_ref):
        o_ref[...] = (a_ref[...].astype(F32) + b_ref[...].astype(F32)).astype(o_ref.dtype)

    spec = pl.BlockSpec((None, tr, cols), lambda q, i: (q, i, 0))
    return pl.pallas_call(
        body, name=name, grid=(s, r // tr), in_specs=[spec, spec], out_specs=spec,
        out_shape=jax.ShapeDtypeStruct((s, r, cols), BF16), compiler_params=_cparams(("parallel", "parallel")),
    )(a, b)


def sum_slots(a, *, name):
    s, r, cols = a.shape
    tr = _slot_rows(r, cols, s + 2)

    def body(a_ref, o_ref):
        acc = a_ref[0].astype(F32)
        for q in range(1, s):
            acc = acc + a_ref[q].astype(F32)
        o_ref[...] = acc

    return pl.pallas_call(
        body, name=name, grid=(r // tr,), in_specs=[pl.BlockSpec((s, tr, cols), lambda i: (0, i, 0))],
        out_specs=pl.BlockSpec((tr, cols), lambda i: (i, 0)),
        out_shape=jax.ShapeDtypeStruct((r, cols), F32), compiler_params=_cparams(("parallel",)),
    )(a)


def reduce_scatter_layer(gs, *, tag):
    c = lax.axis_index("c")
    theirs = sibling_swap_halves(gs, name=f"rs_swap_{tag}")
    parts = []
    for a, (g, th) in enumerate(zip(gs, theirs)):
        rh = g.shape[1] // 2
        own = lax.dynamic_slice_in_dim(g, c * rh, rh, axis=1)
        parts.append(add_slots(own, th, name=f"rs_add{a}_{tag}"))
    landed = chip_scatter(parts, name=f"rs_scatter_{tag}")
    reds = [sum_slots(p, name=f"rs_sum{a}_{tag}") for a, p in enumerate(landed)]
    shared = sibling_share(reds, name=f"rs_share_{tag}")
    return [sh.reshape(g.shape[1], g.shape[2]) for sh, g in zip(shared, gs)]


def adamw(w, g, m, v, *, name):
    r, cols = w.shape
    budget_rows = max(8, (36 * 1024 * 1024) // (14 * 4 * max(cols, LANES)))
    tr = r
    if r % 8 == 0:
        for cand in (1024, 512, 256, 128, 64, 32, 16, 8):
            if cand <= budget_rows and r % cand == 0:
                tr = cand
                break

    def body(w_ref, g_ref, m_ref, v_ref, d_ref, nm_ref, nv_ref):
        gv = g_ref[...]
        nm = ADAM_B1 * m_ref[...] + (1.0 - ADAM_B1) * gv
        nv = ADAM_B2 * v_ref[...] + (1.0 - ADAM_B2) * jnp.square(gv)
        m_hat = nm / (1.0 - ADAM_B1 ** ADAM_STEP)
        v_hat = nv / (1.0 - ADAM_B2 ** ADAM_STEP)
        d_ref[...] = -ADAM_LR * (m_hat / (jnp.sqrt(v_hat) + ADAM_EPS) + ADAM_WD * w_ref[...])
        nm_ref[...] = nm
        nv_ref[...] = nv

    spec = pl.BlockSpec((tr, cols), lambda i: (i, 0))
    shp = jax.ShapeDtypeStruct((r, cols), F32)
    return pl.pallas_call(
        body, name=name, grid=(r // tr,), in_specs=[spec] * 4, out_specs=[spec] * 3, out_shape=[shp] * 3,
        compiler_params=_cparams(("parallel",)),
    )(w, g, m, v)


def adamw_nd(w, g, m, v, *, name):
    shape = w.shape
    two = (lambda a: a.reshape(1, -1)) if w.ndim == 1 else (lambda a: a.reshape(-1, shape[-1]))
    d, nm, nv = adamw(two(w), two(g), two(m), two(v), name=name)
    return d.reshape(shape), nm.reshape(shape), nv.reshape(shape)


def _cols_full(w4):
    return w4.transpose(1, 0, 2).reshape(w4.shape[1], -1)


def _cols_sharded(w):
    return w.reshape(w.shape[0], N_CHIPS, -1).transpose(1, 0, 2)


def _uq_pad(w):
    return jnp.pad(w.reshape(MLA_QL, MLA_H, MLA_NOPE + MLA_ROPE),
                   ((0, 0), (0, 0), (0, MLA_DQK - MLA_NOPE - MLA_ROPE))).reshape(MLA_QL, MLA_CAT_W)


def _uq_unpad(g):
    return g.reshape(MLA_QL, MLA_H, MLA_DQK)[:, :, :MLA_NOPE + MLA_ROPE].reshape(MLA_QL, -1)


def _ukv_split(w):
    return w.reshape(MLA_KVL, MLA_H, 2, MLA_NOPE).transpose(0, 2, 1, 3).reshape(MLA_KVL, -1)


def _ukv_unsplit(g):
    return g.reshape(MLA_KVL, 2, MLA_H, MLA_NOPE).transpose(0, 2, 1, 3).reshape(MLA_KVL, -1)


def _rows(v):
    n = v.shape[0]
    return jnp.pad(v, (0, -n % 1024)).reshape(-1, LANES)


def _fox_bias_views(c):
    ct = c[:, :FOX_H].T
    return ct[:, :, None], ct[:, None, :]


def layer_fwd(x, w, s, tabs, lay, li):
    d = lay.d
    n = lambda k: f"{k}_l{li}"
    sv = {"x": x}
    h1 = rmsnorm_fwd(x, s["g1"], name=n("norm1"))
    proj = mm_nn(h1, w["win"], name=n("proj"))
    ff = mm_nn(h1, w["win"], name=n("proj_ff"), out_dtype=F32, ncols=LANES, col0=lay.ff // LANES)
    c = fox_cumsum(ff, s["bf"], name=n("fox_c"))
    c_col, c_row = _fox_bias_views(c)
    a, lse_a = attention_fwd("fox", proj, proj, proj, name=n("fox_att"), h_n=FOX_H, dk=FOX_DH, dv=FOX_DH,
                             qcb=lay.fq // FOX_DH, kcb=lay.fk // FOX_DH, vcb=lay.fv // FOX_DH,
                             scale=FOX_DH ** -0.5, extra=(c_col, c_row))
    qn = rmsnorm_fwd(proj, s["gq"], name=n("mla_qnorm"), width=MLA_QL, cb=lay.mq // MLA_QL)
    kvn = rmsnorm_fwd(proj, s["gkv"], name=n("mla_kvnorm"), width=MLA_KVL, cb=lay.mkv // MLA_KVL)
    q_pre = mm_nn(qn, w["wuq"], name=n("mla_uq"))
    kv = mm_nn(kvn, w["wukv"], name=n("mla_ukv"))
    q_cat = rope_apply(q_pre, *tabs["mq"], name=n("mla_qrope"), half=MLA_ROPE // 2, groups=MLA_H, width=MLA_CAT_W)
    kr = rope_apply(proj, *tabs["mk"], name=n("mla_krope"), half=MLA_ROPE // 2, groups=1, width=LANES,
                    cb=lay.mkr // LANES)
    k_cat = mla_kcat(kv, kr, name=n("mla_kcat"))
    bm, lse_b = attention_fwd("mla", q_cat, k_cat, kv, name=n("mla_att"), h_n=MLA_H, dk=MLA_DQK, dv=MLA_V,
                              qcb=0, kcb=0, vcb=MLA_H, scale=(MLA_NOPE + MLA_ROPE) ** -0.5)
    rq = rope_apply(proj, *tabs["r"], name=n("ret_qrope"), half=RET_DK // 2, groups=RET_H, width=RET_QK_W,
                    cb=lay.rq // RET_QK_W)
    rk = rope_apply(proj, *tabs["r"], name=n("ret_krope"), half=RET_DK // 2, groups=RET_H, width=RET_QK_W,
                    cb=lay.rk // RET_QK_W, scale=RET_DK ** -0.5)
    on, rstd = attention_fwd("ret", rq, rk, proj, name=n("ret_att"), h_n=RET_H, dk=RET_DK, dv=RET_DV,
                             qcb=0, kcb=0, vcb=lay.rv // RET_DV, extra=(tabs["lgam"],))
    cc = ret_gate(on, proj, lay.rg // RET_V_W, name=n("ret_gate"))
    m0 = mm_nn(a, w["wf"], name=n("br_fox"))
    m1 = mm_nn(bm, w["wm"], name=n("br_mla"))
    m2 = mm_nn(cc, w["wr"], name=n("br_ret"))
    merged = merge_fwd(m0, m1, m2, proj, d, name=n("merge"))
    x1 = mm_nn(merged, w["wo"], name=n("out_proj"), out_dtype=F32, res=x)
    h2 = rmsnorm_fwd(x1, s["g2"], name=n("norm2"))
    u = mm_nn(h2, w["wup"], name=n("ffn_up"))
    gt = mm_nn(h2, w["wg"], name=n("ffn_gate"))
    act = ffn_act(u, gt, s["cw"], s["cb"], name=n("ffn_act"))
    x2 = mm_nn(act, w["wd"], name=n("ffn_down"), out_dtype=F32, res=x1)
    sv.update(h1=h1, proj=proj, ff=ff, c_col=c_col, c_row=c_row, a=a, lse_a=lse_a, qn=qn, kvn=kvn, q_cat=q_cat,
              k_cat=k_cat, kv=kv, bm=bm, lse_b=lse_b, rq=rq, rk=rk, on=on, rstd=rstd, cc=cc, m0=m0, m1=m1, m2=m2,
              merged=merged, x1=x1, h2=h2, u=u, gt=gt, act=act)
    return x2, sv


def layer_bwd(dx2, w, s, tabs, lay, sv, li):
    d = lay.d
    n = lambda k: f"{k}_l{li}"
    proj = sv["proj"]
    dact = mm_nt(dx2, w["wd"], name=n("d_act"))
    dwd = mm_tn(sv["act"], dx2, name=n("dw_down"))
    duc, dgt, dcw8, dcb = ffn_act_bwd(dact, sv["u"], sv["gt"], s["cw"], s["cb"], name=n("ffn_act_bwd"))
    du = conv3_transpose(duc, s["cw"], name=n("ffn_convT"))
    dh2 = mm_nt(du, w["wup"], name=n("d_h2_up"), out_dtype=F32)
    dh2 = mm_nt(dgt, w["wg"], name=n("d_h2_gate"), out_dtype=F32, res=dh2)
    dwup = mm_tn(sv["h2"], du, name=n("dw_up"), shards=N_CHIPS)
    dwg = mm_tn(sv["h2"], dgt, name=n("dw_gate"), shards=N_CHIPS)
    dx1, dg2 = rmsnorm_bwd(sv["x1"], s["g2"], dh2, name=n("norm2_bwd"), res=dx2)
    dmerged = mm_nt(dx1, w["wo"], name=n("d_merged"))
    dwo = mm_tn(sv["merged"], dx1, name=n("dw_out"))
    dm0, dm1, dm2, dgates = merge_bwd(dmerged, sv["m0"], sv["m1"], sv["m2"], proj, d, name=n("merge_bwd"))
    da = mm_nt(dm0, w["wf"], name=n("d_a"))
    dwf = mm_tn(sv["a"], dm0, name=n("dw_br_fox"), shards=N_CHIPS)
    dbm = mm_nt(dm1, w["wm"], name=n("d_bm"))
    dwm = mm_tn(sv["bm"], dm1, name=n("dw_br_mla"), shards=N_CHIPS)
    dcc = mm_nt(dm2, w["wr"], name=n("d_cc"))
    dwr = mm_tn(sv["cc"], dm2, name=n("dw_br_ret"), shards=N_CHIPS)
    drg, do_ret = ret_gate_bwd(dcc, sv["on"], proj, lay.rg // RET_V_W, sv["rstd"], name=n("ret_gate_bwd"))
    dq_r, dk_r, drv = attention_bwd("ret", sv["rq"], sv["rk"], proj, sv["on"], do_ret, sv["rstd"],
                                    name=n("ret_att_bwd"), h_n=RET_H, dk=RET_DK, dv=RET_DV, qcb=0, kcb=0,
                                    vcb=lay.rv // RET_DV, extra=(tabs["lgam"],))
    drq = rope_apply(dq_r, *tabs["r"], name=n("ret_qrope_bwd"), half=RET_DK // 2, groups=RET_H, width=RET_QK_W,
                     transpose=True)
    drk = rope_apply(dk_r, *tabs["r"], name=n("ret_krope_bwd"), half=RET_DK // 2, groups=RET_H, width=RET_QK_W,
                     transpose=True, scale=RET_DK ** -0.5)
    dq_cat, dk_cat, dv_m = attention_bwd("mla", sv["q_cat"], sv["k_cat"], sv["kv"], sv["bm"], dbm, sv["lse_b"],
                                         name=n("mla_att_bwd"), h_n=MLA_H, dk=MLA_DQK, dv=MLA_V, qcb=0, kcb=0,
                                         vcb=MLA_H, scale=(MLA_NOPE + MLA_ROPE) ** -0.5)
    dq_pre = rope_apply(dq_cat, *tabs["mq"], name=n("mla_qrope_bwd"), half=MLA_ROPE // 2, groups=MLA_H,
                        width=MLA_CAT_W, transpose=True)
    dkv, dkr = mla_kcat_bwd(dk_cat, dv_m, name=n("mla_kcat_bwd"))
    dmkr = rope_apply(dkr, *tabs["mk"], name=n("mla_krope_bwd"), half=MLA_ROPE // 2, groups=1, width=LANES,
                      transpose=True)
    dqn = mm_nt(dq_pre, w["wuq"], name=n("d_qn"), out_dtype=F32)
    dwuq = mm_tn(sv["qn"], dq_pre, name=n("dw_uq"))
    dkvn = mm_nt(dkv, w["wukv"], name=n("d_kvn"), out_dtype=F32)
    dwukv = mm_tn(sv["kvn"], dkv, name=n("dw_ukv"))
    dmq, dgq = rmsnorm_bwd(proj, s["gq"], dqn, name=n("mla_qnorm_bwd"), width=MLA_QL, cb=lay.mq // MLA_QL,
                           out_dtype=BF16)
    dmkv, dgkv = rmsnorm_bwd(proj, s["gkv"], dkvn, name=n("mla_kvnorm_bwd"), width=MLA_KVL, cb=lay.mkv // MLA_KVL,
                             out_dtype=BF16)
    dq_f, dk_f, dv_f, dck, dcq = attention_bwd("fox", proj, proj, proj, sv["a"], da, sv["lse_a"], name=n("fox_att_bwd"),
                                          h_n=FOX_H, dk=FOX_DH, dv=FOX_DH, qcb=lay.fq // FOX_DH,
                                          kcb=lay.fk // FOX_DH, vcb=lay.fv // FOX_DH, scale=FOX_DH ** -0.5,
                                          extra=(sv["c_col"], sv["c_row"]))
    dc = jnp.pad((dcq[:, :, 0] - dck[:, 0, :]).T, ((0, 0), (0, LANES - FOX_H)))
    dff, dbf = fox_cumsum_bwd(dc, sv["ff"], s["bf"], name=n("fox_c_bwd"))
    dproj = jnp.concatenate([dgates, drv, drg, dmq, drq, drk, dmkv, dq_f.astype(BF16), dk_f, dv_f, dff, dmkr], axis=1)
    dh1 = mm_nt(dproj, w["win"], name=n("d_h1"), out_dtype=F32)
    dwin = mm_tn(sv["h1"], dproj, name=n("dw_in"))
    dx, dg1 = rmsnorm_bwd(sv["x"], s["g1"], dh1, name=n("norm1_bwd"), res=dx1)
    big = [dwin.reshape(N_CHIPS, d // 4, lay.width),
           _cols_sharded(_uq_unpad(dwuq)), _cols_sharded(_ukv_unsplit(dwukv)), dwf, dwm, dwr,
           dwo.reshape(N_CHIPS, d // 4, d), dwup, dwg, dwd.reshape(N_CHIPS, -1, d)]
    small = [dg1[0], dgq[0], dgkv[0], dbf[0], dg2[0], dcw8[:3].reshape(-1), dcb[0]]
    return dx, big, small


def kernel(x, norm1_g, w_in, mla_q_norm_g, mla_kv_norm_g, mla_w_uq, mla_w_ukv, fox_b_f, w_br_fox, w_br_mla, w_br_ret, w_out, norm2_g, ffn_w_up, ffn_w_gate, ffn_conv_w, ffn_conv_b, ffn_w_down, final_norm_g, loss_target, m_norm1_g, m_w_in, m_mla_q_norm_g, m_mla_kv_norm_g, m_mla_w_uq, m_mla_w_ukv, m_fox_b_f, m_w_br_fox, m_w_br_mla, m_w_br_ret, m_w_out, m_norm2_g, m_ffn_w_up, m_ffn_w_gate, m_ffn_conv_w, m_ffn_conv_b, m_ffn_w_down, m_final_norm_g, v_norm1_g, v_w_in, v_mla_q_norm_g, v_mla_kv_norm_g, v_mla_w_uq, v_mla_w_ukv, v_fox_b_f, v_w_br_fox, v_w_br_mla, v_w_br_ret, v_w_out, v_norm2_g, v_ffn_w_up, v_ffn_w_gate, v_ffn_conv_w, v_ffn_conv_b, v_ffn_w_down, v_final_norm_g):
    names = ["norm1_g", "w_in", "mla_q_norm_g", "mla_kv_norm_g", "mla_w_uq", "mla_w_ukv", "fox_b_f", "w_br_fox",
             "w_br_mla", "w_br_ret", "w_out", "norm2_g", "ffn_w_up", "ffn_w_gate", "ffn_conv_w", "ffn_conv_b",
             "ffn_w_down", "final_norm_g"]
    wts = dict(zip(names, [norm1_g, w_in, mla_q_norm_g, mla_kv_norm_g, mla_w_uq, mla_w_ukv, fox_b_f, w_br_fox,
                           w_br_mla, w_br_ret, w_out, norm2_g, ffn_w_up, ffn_w_gate, ffn_conv_w, ffn_conv_b,
                           ffn_w_down, final_norm_g]))
    mom = dict(zip(names, [m_norm1_g, m_w_in, m_mla_q_norm_g, m_mla_kv_norm_g, m_mla_w_uq, m_mla_w_ukv, m_fox_b_f,
                           m_w_br_fox, m_w_br_mla, m_w_br_ret, m_w_out, m_norm2_g, m_ffn_w_up, m_ffn_w_gate,
                           m_ffn_conv_w, m_ffn_conv_b, m_ffn_w_down, m_final_norm_g]))
    var = dict(zip(names, [v_norm1_g, v_w_in, v_mla_q_norm_g, v_mla_kv_norm_g, v_mla_w_uq, v_mla_w_ukv, v_fox_b_f,
                           v_w_br_fox, v_w_br_mla, v_w_br_ret, v_w_out, v_norm2_g, v_ffn_w_up, v_ffn_w_gate,
                           v_ffn_conv_w, v_ffn_conv_b, v_ffn_w_down, v_final_norm_g]))
    t, d = x.shape[1], x.shape[2]
    depth = w_in.shape[0]
    f = ffn_conv_b.shape[1]
    fs = f // N_CHIPS
    lay = ProjLayout(d)
    chip = 2 * lax.axis_index("x") + lax.axis_index("y")

    tabs = {
        "r": rope_tables(t, RET_DK, RET_DK, 0),
        "mq": rope_tables(t, MLA_ROPE, MLA_DQK, MLA_NOPE),
        "mk": rope_tables(t, MLA_ROPE, LANES, 0),
        "lgam": jnp.log(1.0 - 2.0 ** (-5.0 - jnp.arange(RET_H, dtype=F32))).reshape(RET_H, 1, 1),
    }

    cw_n = depth * 3 * fs
    cw_all = allgather_small(_rows(ffn_conv_w.reshape(-1)), name="ag_conv_w")
    conv_w_full = (cw_all[0::2].reshape(N_CHIPS, -1)[:, :cw_n].reshape(N_CHIPS, depth, 3, fs)
                   .transpose(1, 2, 0, 3).reshape(depth, 3, f))

    def small_params(p):
        return {"g1": p["norm1_g"][None], "gq": p["mla_q_norm_g"][None], "gkv": p["mla_kv_norm_g"][None],
                "bf": jnp.pad(p["fox_b_f"], (0, LANES - FOX_H))[None], "g2": p["norm2_g"][None],
                "cw": p["conv_w_full"], "cb": p["ffn_conv_b"][None]}

    def gathered_weights(p):
        shards = [lay.pad_cols(p["w_in"]), p["mla_w_uq"], p["mla_w_ukv"], p["w_br_fox"], p["w_br_mla"],
                  p["w_br_ret"], p["w_out"], p["ffn_w_up"], p["ffn_w_gate"], p["ffn_w_down"]]
        win, wuq, wukv, wf, wm, wr, wo, wup, wg, wd = allgather_weights([s.astype(BF16) for s in shards],
                                                                        name="ag_layer")
        return {"win": win.reshape(d, lay.width), "wuq": _uq_pad(_cols_full(wuq)), "wukv": _ukv_split(_cols_full(wukv)),
                "wf": wf, "wm": wm, "wr": wr, "wo": wo.reshape(d, d), "wup": wup, "wg": wg, "wd": wd.reshape(f, d)}

    per_layer_in = {k: wts[k] for k in names if k != "final_norm_g"}
    per_layer_in["conv_w_full"] = conv_w_full

    def fwd_layer(xs, p):
        w, s = gathered_weights(p), small_params(p)
        xs, sv = layer_fwd(xs, w, s, tabs, lay, "f")
        return xs, (w, s, sv)

    xs, kept = lax.scan(fwd_layer, x.reshape(t, d), per_layer_in)
    loss_vec, dx, dgf = loss_head(xs, final_norm_g[None], loss_target.reshape(t, d), name="loss_head")

    def bwd_layer(dx, kept_l):
        w, s, sv = kept_l
        dx, big, small = layer_bwd(dx, w, s, tabs, lay, sv, "b")
        return dx, (reduce_scatter_layer(big, tag="layer"), jnp.concatenate(small))

    dx, (reduced, smalls) = lax.scan(bwd_layer, dx, kept, reverse=True)

    sizes = [d, MLA_QL, MLA_KVL, LANES, d, 3 * f, f]
    vec = jnp.concatenate([smalls.reshape(-1), dgf[0], loss_vec[0]])
    tot = sum_slots(allgather_small(_rows(vec), name="ag_small"), name="sum_small").reshape(-1)
    per_layer = sum(sizes)
    sm = {k: [] for k in ("g1", "gq", "gkv", "bf", "g2", "cw", "cb")}
    for li in range(depth):
        off = li * per_layer
        for k, sz in zip(("g1", "gq", "gkv", "bf", "g2", "cw", "cb"), sizes):
            sm[k].append(tot[off:off + sz])
            off += sz
    g_final = tot[depth * per_layer:depth * per_layer + d]
    loss = tot[depth * per_layer + d]

    gwin, guq, gukv, gwf, gwm, gwr, gwo, gwup, gwg, gwd = reduced
    grads = {
        "norm1_g": jnp.stack(sm["g1"]), "w_in": lay.unpad_cols(gwin), "mla_q_norm_g": jnp.stack(sm["gq"]),
        "mla_kv_norm_g": jnp.stack(sm["gkv"]), "mla_w_uq": guq, "mla_w_ukv": gukv,
        "fox_b_f": jnp.stack(sm["bf"])[:, :FOX_H], "w_br_fox": gwf, "w_br_mla": gwm, "w_br_ret": gwr, "w_out": gwo,
        "norm2_g": jnp.stack(sm["g2"]), "ffn_w_up": gwup, "ffn_w_gate": gwg,
        "ffn_conv_w": lax.dynamic_slice_in_dim(jnp.stack(sm["cw"]).reshape(depth, 3, f), chip * fs, fs, axis=2),
        "ffn_conv_b": jnp.stack(sm["cb"]), "ffn_w_down": gwd, "final_norm_g": g_final,
    }
    deltas, new_m, new_v = [], [], []
    for k in names:
        dl, nm, nv = adamw_nd(wts[k], grads[k], mom[k], var[k], name=f"adamw_{k}")
        deltas.append(dl)
        new_m.append(nm)
        new_v.append(nv)
    return (loss, dx.reshape(x.shape), *[grads[k] for k in names], *deltas, *new_m, *new_v)
```

```python
import functools
import math

import jax
import jax.numpy as jnp
import numpy as np
from jax import lax
from jax.experimental import pallas as pl
from jax.experimental.pallas import tpu as pltpu

F32 = jnp.float32
BF16 = jnp.bfloat16
MESH = pl.DeviceIdType.MESH

CHUNK = 64
NORM_EPS = 1e-6
ROPE_THETA = 10000.0
FOX_H, FOX_DH = 6, 128
MLA_H, MLA_NOPE, MLA_ROPE, MLA_V = 6, 128, 64, 128
MLA_QL, MLA_KVL = 512, 256
MLA_DQK = 256
RET_H, RET_DK, RET_DV = 4, 128, 256
FOX_W = FOX_H * FOX_DH
MLA_W = MLA_H * MLA_V
RET_QK_W = RET_H * RET_DK
RET_V_W = RET_H * RET_DV
MLA_CAT_W = MLA_H * MLA_DQK
ADAM_LR, ADAM_B1, ADAM_B2, ADAM_EPS, ADAM_WD, ADAM_STEP = 0.001, 0.9, 0.999, 1e-08, 0.01, 10

N_CHIPS = 4
N_DEV = 8
LANES = 128
VMEM_LIMIT_BYTES = 56 * 1024 * 1024
NEG_BIG = -1e30


def _cparams(sem=None):
    return pltpu.CompilerParams(dimension_semantics=sem, vmem_limit_bytes=VMEM_LIMIT_BYTES)


def _pick(n, prefs):
    for p in prefs:
        if n % p == 0:
            return p
    return n


def _row_tile(t):
    return _pick(t, (256, 128, 64, 32, 16, 8))


def _att_tile(t):
    return 512 if t % 512 == 0 and t >= 2048 else 128


class ProjLayout:
    def __init__(self, d):
        self.d = d
        b = 3 * d
        self.g0 = 0
        self.rv = b
        self.rg = b + 1024
        self.mq = b + 2048
        self.rq = b + 2560
        self.rk = b + 3072
        self.mkv = b + 3584
        self.fq = b + 3840
        self.fk = b + 4608
        self.fv = b + 5376
        self.ff = b + 6144
        self.mkr = b + 6272
        self.width = b + 6400
        o = np.cumsum([0, FOX_W, FOX_W, FOX_W, FOX_H, MLA_QL, MLA_KVL, MLA_ROPE,
                       RET_QK_W, RET_QK_W, RET_V_W, RET_V_W, 3 * d])
        self.orig_width = int(o[-1])
        pads = [self.fq, self.fk, self.fv, self.ff, self.mq, self.mkv, self.mkr,
                self.rq, self.rk, self.rv, self.rg, self.g0]
        self.segs = [(int(o[i]), int(o[i + 1] - o[i]), pads[i]) for i in range(12)]

    def pad_cols(self, w):
        out = jnp.zeros(w.shape[:-1] + (self.width,), w.dtype)
        pieces = sorted(self.segs, key=lambda s: s[2])
        cols, pos = [], 0
        for o, n, p in pieces:
            if p > pos:
                cols.append(jnp.zeros(w.shape[:-1] + (p - pos,), w.dtype))
            cols.append(w[..., o:o + n])
            pos = p + n
        if pos < self.width:
            cols.append(jnp.zeros(w.shape[:-1] + (self.width - pos,), w.dtype))
        del out
        return jnp.concatenate(cols, axis=-1)

    def unpad_cols(self, w):
        return jnp.concatenate([w[..., p:p + n] for o, n, p in self.segs], axis=-1)


_BLK = (2048, 1792, 1536, 1408, 1024, 896, 768, 512, 384, 256, 128)
_BLK_M = (512, 256, 128)
_BLK_K = (512, 256, 128)


def mm_nn(a, b, *, name, out_dtype=BF16, res=None, ncols=None, col0=0):
    m, k = a.shape
    sharded = b.ndim == 3
    if sharded:
        s, _, ns = b.shape
        n = s * ns
        bn = _pick(ns, _BLK)
        nps = ns // bn
    else:
        n = b.shape[1] if ncols is None else ncols
        bn = _pick(n, _BLK)
    bm = _pick(m, _BLK_M)
    bk = _pick(k, _BLK_K)
    nk = k // bk
    cb0 = col0 * (n // bn)

    def body(*refs):
        if res is None:
            a_ref, b_ref, o_ref, acc_ref = refs
        else:
            a_ref, b_ref, r_ref, o_ref, acc_ref = refs
        kk = pl.program_id(2)

        @pl.when(kk == 0)
        def _():
            acc_ref[...] = jnp.zeros_like(acc_ref)

        acc_ref[...] += jnp.dot(a_ref[...].astype(BF16), b_ref[...].astype(BF16), preferred_element_type=F32)

        @pl.when(kk == nk - 1)
        def _():
            r = acc_ref[...]
            if res is not None:
                r = r + r_ref[...]
            o_ref[...] = r.astype(o_ref.dtype)

    if sharded:
        b_spec = pl.BlockSpec((None, bk, bn), lambda i, j, kk: (j // nps, kk, j % nps))
    else:
        b_spec = pl.BlockSpec((bk, bn), lambda i, j, kk: (kk, cb0 + j))
    in_specs = [pl.BlockSpec((bm, bk), lambda i, j, kk: (i, kk)), b_spec]
    args = [a, b]
    if res is not None:
        in_specs.append(pl.BlockSpec((bm, bn), lambda i, j, kk: (i, j)))
        args.append(res)
    return pl.pallas_call(
        body, name=name, grid=(m // bm, n // bn, nk),
        in_specs=in_specs, out_specs=pl.BlockSpec((bm, bn), lambda i, j, kk: (i, j)),
        out_shape=jax.ShapeDtypeStruct((m, n), out_dtype),
        scratch_shapes=[pltpu.VMEM((bm, bn), F32)],
        compiler_params=_cparams(("parallel", "parallel", "arbitrary")),
    )(*args)


def mm_nt(a, b, *, name, out_dtype=BF16, res=None):
    m, n = a.shape
    sharded = b.ndim == 3
    if sharded:
        s, k, ns = b.shape
        bn = _pick(ns, _BLK)
        nps = ns // bn
    else:
        k = b.shape[0]
        bn = _pick(n, _BLK)
    bm = _pick(m, _BLK_M)
    bko = _pick(k, _BLK)
    nn = n // bn

    def body(*refs):
        if res is None:
            a_ref, b_ref, o_ref, acc_ref = refs
        else:
            a_ref, b_ref, r_ref, o_ref, acc_ref = refs
        nidx = pl.program_id(2)

        @pl.when(nidx == 0)
        def _():
            acc_ref[...] = jnp.zeros_like(acc_ref)

        acc_ref[...] += lax.dot_general(a_ref[...].astype(BF16), b_ref[...].astype(BF16),
                                        (((1,), (1,)), ((), ())), preferred_element_type=F32)

        @pl.when(nidx == nn - 1)
        def _():
            r = acc_ref[...]
            if res is not None:
                r = r + r_ref[...]
            o_ref[...] = r.astype(o_ref.dtype)

    if sharded:
        b_spec = pl.BlockSpec((None, bko, bn), lambda i, j, q: (q // nps, j, q % nps))
    else:
        b_spec = pl.BlockSpec((bko, bn), lambda i, j, q: (j, q))
    in_specs = [pl.BlockSpec((bm, bn), lambda i, j, q: (i, q)), b_spec]
    args = [a, b]
    if res is not None:
        in_specs.append(pl.BlockSpec((bm, bko), lambda i, j, q: (i, j)))
        args.append(res)
    return pl.pallas_call(
        body, name=name, grid=(m // bm, k // bko, nn),
        in_specs=in_specs, out_specs=pl.BlockSpec((bm, bko), lambda i, j, q: (i, j)),
        out_shape=jax.ShapeDtypeStruct((m, k), out_dtype),
        scratch_shapes=[pltpu.VMEM((bm, bko), F32)],
        compiler_params=_cparams(("parallel", "parallel", "arbitrary")),
    )(*args)


def mm_tn(a, c, *, name, out_dtype=BF16, shards=1):
    m, k = a.shape
    n = c.shape[1]
    ns = n // shards
    bn = _pick(ns, _BLK)
    nps = ns // bn
    bko = _pick(k, _BLK_K)
    bm = _pick(m, (1024, 512, 256, 128))
    nm = m // bm

    def body(a_ref, c_ref, o_ref, acc_ref):
        mi = pl.program_id(2)

        @pl.when(mi == 0)
        def _():
            acc_ref[...] = jnp.zeros_like(acc_ref)

        acc_ref[...] += lax.dot_general(a_ref[...].astype(BF16), c_ref[...].astype(BF16),
                                        (((0,), (0,)), ((), ())), preferred_element_type=F32)

        @pl.when(mi == nm - 1)
        def _():
            o_ref[...] = acc_ref[...].astype(o_ref.dtype)

    if shards > 1:
        o_spec = pl.BlockSpec((None, bko, bn), lambda i, j, q: (j // nps, i, j % nps))
        o_shape = jax.ShapeDtypeStruct((shards, k, ns), out_dtype)
    else:
        o_spec = pl.BlockSpec((bko, bn), lambda i, j, q: (i, j))
        o_shape = jax.ShapeDtypeStruct((k, n), out_dtype)
    return pl.pallas_call(
        body, name=name, grid=(k // bko, n // bn, nm),
        in_specs=[pl.BlockSpec((bm, bko), lambda i, j, q: (q, i)),
                  pl.BlockSpec((bm, bn), lambda i, j, q: (q, j))],
        out_specs=o_spec, out_shape=o_shape,
        scratch_shapes=[pltpu.VMEM((bko, bn), F32)],
        compiler_params=_cparams(("parallel", "parallel", "arbitrary")),
    )(a, c)


def _rspec(tr, w, cb=0):
    return pl.BlockSpec((tr, w), lambda i: (i, cb))


def _bspec(r, w):
    return pl.BlockSpec((r, w), lambda i: (0, 0))


def _rms_rstd(x):
    return lax.rsqrt(jnp.mean(x * x, axis=-1, keepdims=True) + NORM_EPS)


def rmsnorm_fwd(x, g, *, name, width=None, cb=0):
    t = x.shape[0]
    w = x.shape[1] if width is None else width
    tr = _row_tile(t)

    def body(x_ref, g_ref, o_ref):
        xf = x_ref[...].astype(F32)
        o_ref[...] = (xf * _rms_rstd(xf) * g_ref[...]).astype(o_ref.dtype)

    return pl.pallas_call(
        body, name=name, grid=(t // tr,),
        in_specs=[_rspec(tr, w, cb), _bspec(1, w)], out_specs=_rspec(tr, w),
        out_shape=jax.ShapeDtypeStruct((t, w), BF16), compiler_params=_cparams(("parallel",)),
    )(x, g)


def rmsnorm_bwd(x, g, dy, *, name, width=None, cb=0, res=None, out_dtype=F32):
    t = x.shape[0]
    w = x.shape[1] if width is None else width
    tr = _row_tile(t)

    def body(*refs):
        if res is None:
            x_ref, g_ref, dy_ref, dx_ref, dg_ref = refs
        else:
            x_ref, g_ref, dy_ref, r_ref, dx_ref, dg_ref = refs
        xf = x_ref[...].astype(F32)
        r = _rms_rstd(xf)
        xh = xf * r
        dyf = dy_ref[...].astype(F32)
        dyg = dyf * g_ref[...]
        dx = r * (dyg - xh * jnp.mean(dyg * xh, axis=-1, keepdims=True))
        if res is not None:
            dx = dx + r_ref[...]
        dx_ref[...] = dx.astype(dx_ref.dtype)

        @pl.when(pl.program_id(0) == 0)
        def _():
            dg_ref[...] = jnp.zeros_like(dg_ref)

        dg_ref[...] += jnp.sum(dyf * xh, axis=0, keepdims=True)

    in_specs = [_rspec(tr, w, cb), _bspec(1, w), _rspec(tr, w)]
    args = [x, g, dy]
    if res is not None:
        in_specs.append(_rspec(tr, w))
        args.append(res)
    return pl.pallas_call(
        body, name=name, grid=(t // tr,), in_specs=in_specs,
        out_specs=[_rspec(tr, w), _bspec(1, w)],
        out_shape=[jax.ShapeDtypeStruct((t, w), out_dtype), jax.ShapeDtypeStruct((1, w), F32)],
        compiler_params=_cparams(("arbitrary",)),
    )(*args)


def _log_sigmoid(z):
    return jnp.minimum(z, 0.0) - jnp.log(1.0 + jnp.exp(-jnp.abs(z)))


def _split3(x):
    hi = x.astype(BF16)
    r1 = x - hi.astype(F32)
    mid = r1.astype(BF16)
    lo = (r1 - mid.astype(F32)).astype(BF16)
    return hi, mid, lo


def _tri_dot(tri, x):
    hi, mid, lo = _split3(x)
    d = functools.partial(jnp.dot, preferred_element_type=F32)
    return d(tri, lo) + d(tri, mid) + d(tri, hi)


def fox_cumsum(ff, bias, *, name):
    t = ff.shape[0]
    tr = _row_tile(t)

    def body(f_ref, b_ref, c_ref, carry_ref):
        @pl.when(pl.program_id(0) == 0)
        def _():
            carry_ref[...] = jnp.zeros_like(carry_ref)

        ls = _log_sigmoid(f_ref[...] + b_ref[...])
        row = lax.broadcasted_iota(jnp.int32, (tr, tr), 0)
        col = lax.broadcasted_iota(jnp.int32, (tr, tr), 1)
        tri = (col <= row).astype(BF16)
        c = _tri_dot(tri, ls) + carry_ref[0:1, :]
        c_ref[...] = c
        carry_ref[...] = jnp.broadcast_to(c[tr - 1:tr, :], carry_ref.shape)

    return pl.pallas_call(
        body, name=name, grid=(t // tr,), in_specs=[_rspec(tr, LANES), _bspec(1, LANES)],
        out_specs=_rspec(tr, LANES), out_shape=jax.ShapeDtypeStruct((t, LANES), F32),
        scratch_shapes=[pltpu.VMEM((8, LANES), F32)], compiler_params=_cparams(("arbitrary",)),
    )(ff, bias)


def fox_cumsum_bwd(dc, ff, bias, *, name):
    t = ff.shape[0]
    tr = _row_tile(t)
    n = t // tr

    def body(dc_ref, f_ref, b_ref, df_ref, db_ref, carry_ref):
        @pl.when(pl.program_id(0) == 0)
        def _():
            carry_ref[...] = jnp.zeros_like(carry_ref)
            db_ref[...] = jnp.zeros_like(db_ref)

        row = lax.broadcasted_iota(jnp.int32, (tr, tr), 0)
        col = lax.broadcasted_iota(jnp.int32, (tr, tr), 1)
        tri = (col >= row).astype(BF16)
        dl = _tri_dot(tri, dc_ref[...]) + carry_ref[0:1, :]
        carry_ref[...] = jnp.broadcast_to(dl[0:1, :], carry_ref.shape)
        df = dl * jax.nn.sigmoid(-(f_ref[...] + b_ref[...]))
        df_ref[...] = df.astype(df_ref.dtype)
        db_ref[...] += jnp.sum(df, axis=0, keepdims=True)

    rev = lambda i: (n - 1 - i, 0)
    return pl.pallas_call(
        body, name=name, grid=(n,),
        in_specs=[pl.BlockSpec((tr, LANES), rev), pl.BlockSpec((tr, LANES), rev), _bspec(1, LANES)],
        out_specs=[pl.BlockSpec((tr, LANES), rev), _bspec(1, LANES)],
        out_shape=[jax.ShapeDtypeStruct((t, LANES), BF16), jax.ShapeDtypeStruct((1, LANES), F32)],
        scratch_shapes=[pltpu.VMEM((8, LANES), F32)], compiler_params=_cparams(("arbitrary",)),
    )(dc, ff, bias)


def _rot(x, half):
    w = x.shape[-1]
    lane = lax.broadcasted_iota(jnp.int32, x.shape, x.ndim - 1)
    first = (lane % (2 * half)) < half
    return jnp.where(first, pltpu.roll(x, w - half, x.ndim - 1), pltpu.roll(x, half, x.ndim - 1))


def rope_tables(t, d, width, lo):
    pos = jnp.arange(t, dtype=F32)
    inv_freq = ROPE_THETA ** (-jnp.arange(0, d, 2, dtype=F32) / d)
    ang = pos[:, None] * inv_freq[None, :]
    cos, sin = jnp.cos(ang), jnp.sin(ang)
    c = jnp.ones((t, width), F32).at[:, lo:lo + d].set(jnp.concatenate([cos, cos], axis=1))
    s = jnp.zeros((t, width), F32).at[:, lo:lo + d].set(jnp.concatenate([-sin, sin], axis=1))
    return c, s


def rope_apply(x, cos, sin, *, name, half, groups, width, cb=0, scale=1.0, transpose=False, out_dtype=BF16):
    t = x.shape[0]
    tr = _row_tile(t)
    gw = cos.shape[1]
    assert gw * groups == width
    sgn = -1.0 if transpose else 1.0

    def body(x_ref, c_ref, s_ref, o_ref):
        c, s = c_ref[...], s_ref[...]
        for g in range(groups):
            xs = x_ref[:, g * gw:(g + 1) * gw].astype(F32)
            y = xs * c + sgn * _rot(xs, half) * s
            o_ref[:, g * gw:(g + 1) * gw] = (y * scale).astype(o_ref.dtype)

    return pl.pallas_call(
        body, name=name, grid=(t // tr,),
        in_specs=[_rspec(tr, width, cb), _rspec(tr, gw), _rspec(tr, gw)], out_specs=_rspec(tr, width),
        out_shape=jax.ShapeDtypeStruct((t, width), out_dtype), compiler_params=_cparams(("parallel",)),
    )(x, cos, sin)


def mla_kcat(kv, kr, *, name):
    t = kv.shape[0]
    tr = _row_tile(t)

    def body(kv_ref, kr_ref, o_ref):
        krv = kr_ref[...]
        for h in range(MLA_H):
            o_ref[:, h * MLA_DQK:h * MLA_DQK + MLA_NOPE] = kv_ref[:, h * MLA_NOPE:(h + 1) * MLA_NOPE]
            o_ref[:, h * MLA_DQK + MLA_NOPE:(h + 1) * MLA_DQK] = krv

    return pl.pallas_call(
        body, name=name, grid=(t // tr,),
        in_specs=[_rspec(tr, MLA_H * MLA_NOPE, 0), _rspec(tr, LANES)], out_specs=_rspec(tr, MLA_CAT_W),
        out_shape=jax.ShapeDtypeStruct((t, MLA_CAT_W), BF16), compiler_params=_cparams(("parallel",)),
    )(kv, kr)


def mla_kcat_bwd(dk_cat, dv, *, name):
    t = dk_cat.shape[0]
    tr = _row_tile(t)

    def body(dk_ref, dv_ref, dkv_ref, dkr_ref):
        acc = jnp.zeros((tr, LANES), F32)
        for h in range(MLA_H):
            dkv_ref[:, h * MLA_NOPE:(h + 1) * MLA_NOPE] = dk_ref[:, h * MLA_DQK:h * MLA_DQK + MLA_NOPE].astype(BF16)
            acc = acc + dk_ref[:, h * MLA_DQK + MLA_NOPE:(h + 1) * MLA_DQK].astype(F32)
        dkv_ref[:, MLA_H * MLA_NOPE:] = dv_ref[...].astype(BF16)
        dkr_ref[...] = acc

    return pl.pallas_call(
        body, name=name, grid=(t // tr,),
        in_specs=[_rspec(tr, MLA_CAT_W), _rspec(tr, MLA_W)],
        out_specs=[_rspec(tr, MLA_CAT_W), _rspec(tr, LANES)],
        out_shape=[jax.ShapeDtypeStruct((t, MLA_CAT_W), BF16), jax.ShapeDtypeStruct((t, LANES), F32)],
        compiler_params=_cparams(("parallel",)),
    )(dk_cat, dv)


def _silu(g):
    return g * jax.nn.sigmoid(g)


def ret_gate(on, proj, rg_cb, *, name):
    t = on.shape[0]
    tr = _row_tile(t)

    def body(on_ref, g_ref, o_ref):
        o_ref[...] = (on_ref[...].astype(F32) * _silu(g_ref[...].astype(F32))).astype(o_ref.dtype)

    return pl.pallas_call(
        body, name=name, grid=(t // tr,), in_specs=[_rspec(tr, RET_V_W), _rspec(tr, RET_V_W, rg_cb)],
        out_specs=_rspec(tr, RET_V_W), out_shape=jax.ShapeDtypeStruct((t, RET_V_W), BF16),
        compiler_params=_cparams(("parallel",)),
    )(on, proj)


def ret_gate_bwd(dy, on, proj, rg_cb, rstd, *, name):
    t = on.shape[0]
    tr = _row_tile(t)

    def body(dy_ref, on_ref, g_ref, r_ref, dg_ref, do_ref):
        dyf = dy_ref[...].astype(F32)
        onf = on_ref[...].astype(F32)
        g = g_ref[...].astype(F32)
        sg = jax.nn.sigmoid(g)
        dg_ref[...] = (dyf * onf * sg * (1.0 + g * (1.0 - sg))).astype(dg_ref.dtype)
        don = dyf * g * sg
        for h in range(RET_H):
            sl = slice(h * RET_DV, (h + 1) * RET_DV)
            d_h, o_h = don[:, sl], onf[:, sl]
            do_ref[:, sl] = (r_ref[h] * (d_h - o_h * jnp.mean(d_h * o_h, axis=-1, keepdims=True))).astype(do_ref.dtype)

    return pl.pallas_call(
        body, name=name, grid=(t // tr,),
        in_specs=[_rspec(tr, RET_V_W), _rspec(tr, RET_V_W), _rspec(tr, RET_V_W, rg_cb),
                  pl.BlockSpec((RET_H, tr, 1), lambda i: (0, i, 0))],
        out_specs=[_rspec(tr, RET_V_W), _rspec(tr, RET_V_W)],
        out_shape=[jax.ShapeDtypeStruct((t, RET_V_W), BF16), jax.ShapeDtypeStruct((t, RET_V_W), BF16)],
        compiler_params=_cparams(("parallel",)),
    )(dy, on, proj, rstd)


def merge_fwd(m0, m1, m2, proj, d, *, name):
    t = m0.shape[0]
    tr = _row_tile(t)

    def body(m0_ref, m1_ref, m2_ref, g0_ref, g1_ref, g2_ref, o_ref):
        acc = jax.nn.sigmoid(g0_ref[...].astype(F32)) * m0_ref[...].astype(F32)
        acc = acc + jax.nn.sigmoid(g1_ref[...].astype(F32)) * m1_ref[...].astype(F32)
        acc = acc + jax.nn.sigmoid(g2_ref[...].astype(F32)) * m2_ref[...].astype(F32)
        o_ref[...] = acc.astype(o_ref.dtype)

    return pl.pallas_call(
        body, name=name, grid=(t // tr,),
        in_specs=[_rspec(tr, d)] * 3 + [_rspec(tr, d, 0), _rspec(tr, d, 1), _rspec(tr, d, 2)],
        out_specs=_rspec(tr, d), out_shape=jax.ShapeDtypeStruct((t, d), BF16),
        compiler_params=_cparams(("parallel",)),
    )(m0, m1, m2, proj, proj, proj)


def merge_bwd(dm, m0, m1, m2, proj, d, *, name):
    t = m0.shape[0]
    tr = _row_tile(t)

    def body(dm_ref, m0_ref, m1_ref, m2_ref, g0_ref, g1_ref, g2_ref, d0_ref, d1_ref, d2_ref, dg_ref):
        dmf = dm_ref[...].astype(F32)
        for i, (m_ref, g_ref, d_ref) in enumerate(((m0_ref, g0_ref, d0_ref), (m1_ref, g1_ref, d1_ref),
                                                   (m2_ref, g2_ref, d2_ref))):
            sg = jax.nn.sigmoid(g_ref[...].astype(F32))
            d_ref[...] = (dmf * sg).astype(d_ref.dtype)
            dg_ref[:, i * d:(i + 1) * d] = (dmf * m_ref[...].astype(F32) * sg * (1.0 - sg)).astype(dg_ref.dtype)

    return pl.pallas_call(
        body, name=name, grid=(t // tr,),
        in_specs=[_rspec(tr, d)] * 4 + [_rspec(tr, d, 0), _rspec(tr, d, 1), _rspec(tr, d, 2)],
        out_specs=[_rspec(tr, d)] * 3 + [_rspec(tr, 3 * d)],
        out_shape=[jax.ShapeDtypeStruct((t, d), BF16)] * 3 + [jax.ShapeDtypeStruct((t, 3 * d), BF16)],
        compiler_params=_cparams(("parallel",)),
    )(dm, m0, m1, m2, proj, proj, proj)


_GELU_K = math.sqrt(2.0 / math.pi)
_GELU_C = 0.044715


def _gelu(x):
    return 0.5 * x * (1.0 + jnp.tanh(_GELU_K * (x + _GELU_C * x * x * x)))


def _gelu_grad(x):
    th = jnp.tanh(_GELU_K * (x + _GELU_C * x * x * x))
    return 0.5 * (1.0 + th) + 0.5 * x * (1.0 - th * th) * _GELU_K * (1.0 + 3.0 * _GELU_C * x * x)


def _shift_down(u, halo, k, first):
    tr = u.shape[0]
    r = pltpu.roll(u, k, 0)
    row = lax.broadcasted_iota(jnp.int32, u.shape, 0)
    for q in range(k):
        h = jnp.where(first, 0.0, halo[8 - k + q:8 - k + q + 1, :])
        r = jnp.where(row == q, h, r)
    del tr
    return r


def _shift_up(u, halo, k, last):
    tr = u.shape[0]
    r = pltpu.roll(u, tr - k, 0)
    row = lax.broadcasted_iota(jnp.int32, u.shape, 0)
    for q in range(k):
        h = jnp.where(last, 0.0, halo[q:q + 1, :])
        r = jnp.where(row == tr - k + q, h, r)
    return r


def _conv3(u, halo, w_ref, b_ref, first):
    return (b_ref[...] + w_ref[0:1, :] * _shift_down(u, halo, 2, first)
            + w_ref[1:2, :] * _shift_down(u, halo, 1, first) + w_ref[2:3, :] * u)


def _ffn_tiles(t, f):
    tr = _row_tile(t)
    cw = f // 4 if (f // 4) % LANES == 0 else f
    return tr, cw


def ffn_act(u, gt, conv_w, conv_b, *, name):
    t, f = u.shape
    tr, cw = _ffn_tiles(t, f)
    hb = tr // 8

    def body(u_ref, h_ref, g_ref, w_ref, b_ref, o_ref):
        first = pl.program_id(1) == 0
        uc = _conv3(u_ref[...].astype(F32), h_ref[...].astype(F32), w_ref, b_ref, first)
        o_ref[...] = (_gelu(uc) * g_ref[...].astype(F32)).astype(o_ref.dtype)

    return pl.pallas_call(
        body, name=name, grid=(f // cw, t // tr),
        in_specs=[pl.BlockSpec((tr, cw), lambda j, i: (i, j)),
                  pl.BlockSpec((8, cw), lambda j, i: (jnp.maximum(i * hb - 1, 0), j)),
                  pl.BlockSpec((tr, cw), lambda j, i: (i, j)),
                  pl.BlockSpec((3, cw), lambda j, i: (0, j)), pl.BlockSpec((1, cw), lambda j, i: (0, j))],
        out_specs=pl.BlockSpec((tr, cw), lambda j, i: (i, j)),
        out_shape=jax.ShapeDtypeStruct((t, f), BF16), compiler_params=_cparams(("parallel", "parallel")),
    )(u, u, gt, conv_w, conv_b)


def ffn_act_bwd(dact, u, gt, conv_w, conv_b, *, name):
    t, f = u.shape
    tr, cw = _ffn_tiles(t, f)
    hb = tr // 8

    def body(da_ref, u_ref, h_ref, g_ref, w_ref, b_ref, duc_ref, dg_ref, dw_ref, db_ref):
        first = pl.program_id(1) == 0

        @pl.when(first)
        def _():
            dw_ref[...] = jnp.zeros_like(dw_ref)
            db_ref[...] = jnp.zeros_like(db_ref)

        uf, hf = u_ref[...].astype(F32), h_ref[...].astype(F32)
        uc = _conv3(uf, hf, w_ref, b_ref, first)
        da = da_ref[...].astype(F32)
        dg_ref[...] = (da * _gelu(uc)).astype(dg_ref.dtype)
        duc = da * g_ref[...].astype(F32) * _gelu_grad(uc)
        duc_ref[...] = duc.astype(duc_ref.dtype)
        db_ref[...] += jnp.sum(duc, axis=0, keepdims=True)
        dw_ref[0:1, :] += jnp.sum(duc * _shift_down(uf, hf, 2, first), axis=0, keepdims=True)
        dw_ref[1:2, :] += jnp.sum(duc * _shift_down(uf, hf, 1, first), axis=0, keepdims=True)
        dw_ref[2:3, :] += jnp.sum(duc * uf, axis=0, keepdims=True)

    tile = pl.BlockSpec((tr, cw), lambda j, i: (i, j))
    return pl.pallas_call(
        body, name=name, grid=(f // cw, t // tr),
        in_specs=[tile, tile, pl.BlockSpec((8, cw), lambda j, i: (jnp.maximum(i * hb - 1, 0), j)), tile,
                  pl.BlockSpec((3, cw), lambda j, i: (0, j)), pl.BlockSpec((1, cw), lambda j, i: (0, j))],
        out_specs=[tile, tile, pl.BlockSpec((8, cw), lambda j, i: (0, j)), pl.BlockSpec((1, cw), lambda j, i: (0, j))],
        out_shape=[jax.ShapeDtypeStruct((t, f), BF16), jax.ShapeDtypeStruct((t, f), BF16),
                   jax.ShapeDtypeStruct((8, f), F32), jax.ShapeDtypeStruct((1, f), F32)],
        compiler_params=_cparams(("parallel", "arbitrary")),
    )(dact, u, u, gt, conv_w, conv_b)


def conv3_transpose(duc, conv_w, *, name):
    t, f = duc.shape
    tr, cw = _ffn_tiles(t, f)
    hb = tr // 8
    nt = t // tr

    def body(d_ref, h_ref, w_ref, o_ref):
        last = pl.program_id(1) == nt - 1
        d, h = d_ref[...].astype(F32), h_ref[...].astype(F32)
        o = w_ref[2:3, :] * d + w_ref[1:2, :] * _shift_up(d, h, 1, last) + w_ref[0:1, :] * _shift_up(d, h, 2, last)
        o_ref[...] = o.astype(o_ref.dtype)

    tile = pl.BlockSpec((tr, cw), lambda j, i: (i, j))
    return pl.pallas_call(
        body, name=name, grid=(f // cw, nt),
        in_specs=[tile, pl.BlockSpec((8, cw), lambda j, i: (jnp.minimum((i + 1) * hb, t // 8 - 1), j)),
                  pl.BlockSpec((3, cw), lambda j, i: (0, j))],
        out_specs=tile, out_shape=jax.ShapeDtypeStruct((t, f), BF16),
        compiler_params=_cparams(("parallel", "parallel")),
    )(duc, duc, conv_w)


def loss_head(x, g, target, *, name):
    t, d = x.shape
    tr = _row_tile(t)

    def body(x_ref, g_ref, t_ref, l_ref, dx_ref, dg_ref):
        @pl.when(pl.program_id(0) == 0)
        def _():
            l_ref[...] = jnp.zeros_like(l_ref)
            dg_ref[...] = jnp.zeros_like(dg_ref)

        xf = x_ref[...]
        r = _rms_rstd(xf)
        xh = xf * r
        err = xh * g_ref[...] - t_ref[...]
        l_ref[...] += 0.5 * jnp.sum(jnp.mean(err * err, axis=-1, keepdims=True), axis=0, keepdims=True)
        dy = err / d
        dyg = dy * g_ref[...]
        dx_ref[...] = r * (dyg - xh * jnp.mean(dyg * xh, axis=-1, keepdims=True))
        dg_ref[...] += jnp.sum(dy * xh, axis=0, keepdims=True)

    return pl.pallas_call(
        body, name=name, grid=(t // tr,),
        in_specs=[_rspec(tr, d), _bspec(1, d), _rspec(tr, d)],
        out_specs=[_bspec(1, LANES), _rspec(tr, d), _bspec(1, d)],
        out_shape=[jax.ShapeDtypeStruct((1, LANES), F32), jax.ShapeDtypeStruct((t, d), F32),
                   jax.ShapeDtypeStruct((1, d), F32)],
        compiler_params=_cparams(("arbitrary",)),
    )(x, g, target)


def _att_mask(mode, i, j, tq, tk):
    t_pos = i * tq + lax.broadcasted_iota(jnp.int32, (tq, tk), 0)
    s_pos = j * tk + lax.broadcasted_iota(jnp.int32, (tq, tk), 1)
    if mode == "fox":
        return s_pos <= t_pos, t_pos, s_pos
    return (s_pos // CHUNK) <= (t_pos // CHUNK), t_pos, s_pos


def _nt(a, b):
    return lax.dot_general(a, b, (((1,), (1,)), ((), ())), preferred_element_type=F32)


def _tn(a, b):
    return lax.dot_general(a, b, (((0,), (0,)), ((), ())), preferred_element_type=F32)


def _scores(mode, scale, q, k, extra, i, j, tq, tk, diag):
    s = _nt(q, k)
    if mode == "fox":
        cq_ref, ck_ref = extra
        s = s * scale + cq_ref[...] - ck_ref[...]
    elif mode == "mla":
        s = s * scale
    if mode != "ret" and not diag:
        return s, None, None
    mask, t_pos, s_pos = _att_mask(mode, i, j, tq, tk)
    if mode != "ret":
        return jnp.where(mask, s, NEG_BIG), mask, None
    (lg_ref,) = extra
    dec = jnp.exp(lg_ref[...] * jnp.abs(t_pos - s_pos).astype(F32))
    if diag:
        dec = jnp.where(mask, dec, 0.0)
    return s * dec, None, dec


def attention_fwd(mode, q, k, v, *, name, h_n, dk, dv, qcb, kcb, vcb, scale=1.0, extra=()):
    t = q.shape[0]
    tq = tk = _att_tile(t)
    nq = t // tq
    softmax = mode != "ret"

    def body(*refs):
        q_ref, k_ref, v_ref = refs[:3]
        n_ex = len(extra)
        ex = refs[3:3 + n_ex]
        o_ref, st_ref, m_ref, l_ref, acc_ref = refs[3 + n_ex:]
        i, j = pl.program_id(1), pl.program_id(2)

        @pl.when(j == 0)
        def _():
            m_ref[...] = jnp.full_like(m_ref, NEG_BIG)
            l_ref[...] = jnp.zeros_like(l_ref)
            acc_ref[...] = jnp.zeros_like(acc_ref)

        def step(diag):
            s, _, _ = _scores(mode, scale, q_ref[...], k_ref[...], ex, i, j, tq, tk, diag)
            if softmax:
                m_old = m_ref[...]
                m_new = jnp.maximum(m_old, jnp.max(s, axis=-1, keepdims=True))
                p = jnp.exp(s - m_new)
                alpha = jnp.exp(m_old - m_new)
                l_ref[...] = alpha * l_ref[...] + jnp.sum(p, axis=-1, keepdims=True)
                acc_ref[...] = alpha * acc_ref[...] + jnp.dot(p.astype(BF16), v_ref[...], preferred_element_type=F32)
                m_ref[...] = m_new
            else:
                acc_ref[...] += jnp.dot(s.astype(BF16), v_ref[...], preferred_element_type=F32)

        @pl.when(j < i)
        def _():
            step(False)

        @pl.when(j == i)
        def _():
            step(True)
            if softmax:
                o_ref[...] = (acc_ref[...] / l_ref[...]).astype(o_ref.dtype)
                st_ref[...] = m_ref[...] + jnp.log(l_ref[...])
            else:
                o = acc_ref[...]
                r = lax.rsqrt(jnp.mean(o * o, axis=-1, keepdims=True) + NORM_EPS)
                o_ref[...] = (o * r).astype(o_ref.dtype)
                st_ref[...] = r

    in_specs = [pl.BlockSpec((tq, dk), lambda h, i, j: (i, qcb + h)),
                pl.BlockSpec((tk, dk), lambda h, i, j: (jnp.minimum(j, i), kcb + h)),
                pl.BlockSpec((tk, dv), lambda h, i, j: (jnp.minimum(j, i), vcb + h))]
    if mode == "fox":
        in_specs += [pl.BlockSpec((None, tq, 1), lambda h, i, j: (h, i, 0)),
                     pl.BlockSpec((None, 1, tk), lambda h, i, j: (h, 0, jnp.minimum(j, i)))]
    elif mode == "ret":
        in_specs += [pl.BlockSpec((None, 1, 1), lambda h, i, j: (h, 0, 0))]
    return pl.pallas_call(
        body, name=name, grid=(h_n, nq, nq), in_specs=in_specs,
        out_specs=[pl.BlockSpec((tq, dv), lambda h, i, j: (i, h)),
                   pl.BlockSpec((None, tq, 1), lambda h, i, j: (h, i, 0))],
        out_shape=[jax.ShapeDtypeStruct((t, h_n * dv), BF16), jax.ShapeDtypeStruct((h_n, t, 1), F32)],
        scratch_shapes=[pltpu.VMEM((tq, 1), F32), pltpu.VMEM((tq, 1), F32), pltpu.VMEM((tq, dv), F32)],
        compiler_params=_cparams(("parallel", "parallel", "arbitrary")),
    )(q, k, v, *extra)


def attention_bwd(mode, q, k, v, o, do, stat, *, name, h_n, dk, dv, qcb, kcb, vcb, scale=1.0, extra=()):
    t = q.shape[0]
    tq = tk = _att_tile(t)
    nq = t // tq
    softmax = mode != "ret"

    def body(*refs):
        q_ref, k_ref, v_ref, o_ref, do_ref, st_ref = refs[:6]
        n_ex = len(extra)
        ex = refs[6:6 + n_ex]
        outs = refs[6 + n_ex:]
        if mode == "fox":
            dq_ref, dk_ref, dv_ref, dck_ref, dcq_ref, dk_acc, dv_acc, dck_acc = outs
        else:
            dq_ref, dk_ref, dv_ref, dk_acc, dv_acc = outs
        j, i = pl.program_id(1), pl.program_id(2)

        @pl.when((j == 0) & (i == 0))
        def _():
            dq_ref[...] = jnp.zeros_like(dq_ref)
            if mode == "fox":
                dcq_ref[...] = jnp.zeros_like(dcq_ref)

        @pl.when(i == j)
        def _():
            dk_acc[...] = jnp.zeros_like(dk_acc)
            dv_acc[...] = jnp.zeros_like(dv_acc)
            if mode == "fox":
                dck_acc[...] = jnp.zeros_like(dck_acc)

        def step(diag):
            qv, kv_, vv, dov = q_ref[...], k_ref[...], v_ref[...], do_ref[...]
            s, mask, dec = _scores(mode, scale, qv, kv_, ex, i, j, tq, tk, diag)
            dp = _nt(dov, vv)
            if softmax:
                p = jnp.exp(s - st_ref[...])
                if mask is not None:
                    p = jnp.where(mask, p, 0.0)
                delta = jnp.sum(dov.astype(F32) * o_ref[...].astype(F32), axis=-1, keepdims=True)
                ds = p * (dp - delta)
            else:
                p = s
                ds = dp * dec
            dv_acc[...] += _tn(p.astype(BF16), dov)
            dsb = ds.astype(BF16)
            dk_acc[...] += _tn(dsb, qv)
            rows = pl.ds(pl.multiple_of(i * tq, tq), tq)
            dq_ref[rows, :] += jnp.dot(dsb, kv_, preferred_element_type=F32) * scale
            if mode == "fox":
                dck_acc[...] += jnp.sum(ds, axis=0, keepdims=True)
                dcq_ref[rows, :] += jnp.sum(ds, axis=1, keepdims=True)

        @pl.when(i == j)
        def _():
            step(True)

        @pl.when(i > j)
        def _():
            step(False)

        @pl.when(i == nq - 1)
        def _():
            dk_ref[...] = (dk_acc[...] * scale).astype(dk_ref.dtype)
            dv_ref[...] = dv_acc[...].astype(dv_ref.dtype)
            if mode == "fox":
                dck_ref[...] = dck_acc[...]

    qi = lambda h, j, i: (jnp.maximum(i, j), qcb + h)
    in_specs = [pl.BlockSpec((tq, dk), qi),
                pl.BlockSpec((tk, dk), lambda h, j, i: (j, kcb + h)),
                pl.BlockSpec((tk, dv), lambda h, j, i: (j, vcb + h)),
                pl.BlockSpec((tq, dv), lambda h, j, i: (jnp.maximum(i, j), h)),
                pl.BlockSpec((tq, dv), lambda h, j, i: (jnp.maximum(i, j), h)),
                pl.BlockSpec((None, tq, 1), lambda h, j, i: (h, jnp.maximum(i, j), 0))]
    out_specs = [pl.BlockSpec((t, dk), lambda h, j, i: (0, h)),
                 pl.BlockSpec((tk, dk), lambda h, j, i: (j, h)),
                 pl.BlockSpec((tk, dv), lambda h, j, i: (j, h))]
    out_shape = [jax.ShapeDtypeStruct((t, h_n * dk), F32), jax.ShapeDtypeStruct((t, h_n * dk), BF16),
                 jax.ShapeDtypeStruct((t, h_n * dv), BF16)]
    scratch = [pltpu.VMEM((tk, dk), F32), pltpu.VMEM((tk, dv), F32)]
    if mode == "fox":
        in_specs += [pl.BlockSpec((None, tq, 1), lambda h, j, i: (h, jnp.maximum(i, j), 0)),
                     pl.BlockSpec((None, 1, tk), lambda h, j, i: (h, 0, j))]
        out_specs += [pl.BlockSpec((None, 1, tk), lambda h, j, i: (h, 0, j)),
                      pl.BlockSpec((None, t, 1), lambda h, j, i: (h, 0, 0))]
        out_shape += [jax.ShapeDtypeStruct((h_n, 1, t), F32), jax.ShapeDtypeStruct((h_n, t, 1), F32)]
        scratch.append(pltpu.VMEM((1, tk), F32))
    elif mode == "ret":
        in_specs += [pl.BlockSpec((None, 1, 1), lambda h, j, i: (h, 0, 0))]
    return pl.pallas_call(
        body, name=name, grid=(h_n, nq, nq), in_specs=in_specs, out_specs=out_specs, out_shape=out_shape,
        scratch_shapes=scratch, compiler_params=_cparams(("parallel", "arbitrary", "arbitrary")),
    )(q, k, v, o, do, stat, *extra)


_HBM = pl.BlockSpec(memory_space=pltpu.HBM)
_VMEM = pl.BlockSpec(memory_space=pltpu.VMEM)


def _place():
    x, y, c = lax.axis_index("x"), lax.axis_index("y"), lax.axis_index("c")
    chips = [(1 - x, y), (x, 1 - y), (1 - x, 1 - y)]
    return x, y, c, chips


def allgather_weights(shards, *, name):
    na = len(shards)

    def body(*refs):
        x_refs, out_refs = refs[:na], refs[na:2 * na]
        send_sems, recv_sems, local_sems = refs[2 * na:]
        x, y, c, chips = _place()
        my = 2 * x + y
        me, sibling = (x, y, c), (x, y, 1 - c)

        def blk(a, chip, half):
            rh = shards[a].shape[0] // 2
            return out_refs[a].at[chip, pl.ds(half * rh, rh), :]

        def copy(a, k, src, dst, to):
            return pltpu.make_async_remote_copy(src_ref=src, dst_ref=dst, send_sem=send_sems.at[6 * a + k],
                                                recv_sem=recv_sems.at[6 * a + k], device_id=to, device_id_type=MESH)

        mine, first, passed = [], [], []
        for a in range(na):
            rh = shards[a].shape[0] // 2
            cp = pltpu.make_async_copy(x_refs[a], out_refs[a].at[my], local_sems.at[a])
            cp.start()
            mine.append(cp)
            for j, chip in enumerate(chips):
                cp = copy(a, j, x_refs[a].at[pl.ds(c * rh, rh), :], blk(a, my, c), (*chip, c))
                cp.start()
                first.append(cp)
        for a in range(na):
            for j, (px, py) in enumerate(chips):
                got = blk(a, 2 * px + py, c)
                copy(a, j, got, got, me).wait_recv()
                cp = copy(a, 3 + j, got, got, sibling)
                cp.start()
                passed.append(cp)
        for a in range(na):
            for j, (px, py) in enumerate(chips):
                got = blk(a, 2 * px + py, 1 - c)
                copy(a, 3 + j, got, got, me).wait_recv()
        for cp in first + passed:
            cp.wait_send()
        for cp in mine:
            cp.wait()

    return pl.pallas_call(
        body, name=name, in_specs=[_HBM] * na, out_specs=[_HBM] * na,
        out_shape=[jax.ShapeDtypeStruct((N_CHIPS,) + s.shape, s.dtype) for s in shards],
        scratch_shapes=[pltpu.SemaphoreType.DMA((6 * na,)), pltpu.SemaphoreType.DMA((6 * na,)),
                        pltpu.SemaphoreType.DMA((na,))],
    )(*shards)


def sibling_swap_halves(gs, *, name):
    na = len(gs)

    def body(*refs):
        g_refs, out_refs = refs[:na], refs[na:2 * na]
        send_sems, recv_sems = refs[2 * na:]
        x, y, c, _ = _place()
        cps = []
        for a in range(na):
            rh = gs[a].shape[1] // 2
            cp = pltpu.make_async_remote_copy(
                src_ref=g_refs[a].at[:, pl.ds((1 - c) * rh, rh), :], dst_ref=out_refs[a], send_sem=send_sems.at[a],
                recv_sem=recv_sems.at[a], device_id=(x, y, 1 - c), device_id_type=MESH)
            cp.start()
            cps.append(cp)
        for cp in cps:
            cp.wait()

    return pl.pallas_call(
        body, name=name, in_specs=[_HBM] * na, out_specs=[_HBM] * na,
        out_shape=[jax.ShapeDtypeStruct((g.shape[0], g.shape[1] // 2, g.shape[2]), g.dtype) for g in gs],
        scratch_shapes=[pltpu.SemaphoreType.DMA((na,)), pltpu.SemaphoreType.DMA((na,))],
    )(*gs)


def chip_scatter(ps, *, name):
    na = len(ps)

    def body(*refs):
        p_refs, out_refs = refs[:na], refs[na:2 * na]
        send_sems, recv_sems, local_sems = refs[2 * na:]
        x, y, c, chips = _place()
        my = 2 * x + y
        mine, sends = [], []
        for a in range(na):
            cp = pltpu.make_async_copy(p_refs[a].at[my], out_refs[a].at[my], local_sems.at[a])
            cp.start()
            mine.append(cp)
            for j, (px, py) in enumerate(chips):
                cp = pltpu.make_async_remote_copy(
                    src_ref=p_refs[a].at[2 * px + py], dst_ref=out_refs[a].at[my], send_sem=send_sems.at[3 * a + j],
                    recv_sem=recv_sems.at[3 * a + j], device_id=(px, py, c), device_id_type=MESH)
                cp.start()
                sends.append(cp)
        for a in range(na):
            for j, (px, py) in enumerate(chips):
                got = out_refs[a].at[2 * px + py]
                pltpu.make_async_remote_copy(src_ref=got, dst_ref=got, send_sem=send_sems.at[3 * a + j],
                                             recv_sem=recv_sems.at[3 * a + j], device_id=(x, y, c),
                                             device_id_type=MESH).wait_recv()
        for cp in sends:
            cp.wait_send()
        for cp in mine:
            cp.wait()

    return pl.pallas_call(
        body, name=name, in_specs=[_HBM] * na, out_specs=[_HBM] * na,
        out_shape=[jax.ShapeDtypeStruct(p.shape, p.dtype) for p in ps],
        scratch_shapes=[pltpu.SemaphoreType.DMA((3 * na,)), pltpu.SemaphoreType.DMA((3 * na,)),
                        pltpu.SemaphoreType.DMA((na,))],
    )(*ps)


def sibling_share(reds, *, name):
    na = len(reds)

    def body(*refs):
        r_refs, out_refs = refs[:na], refs[na:2 * na]
        send_sems, recv_sems, local_sems = refs[2 * na:]
        x, y, c, _ = _place()
        mine, sends = [], []
        for a in range(na):
            cp = pltpu.make_async_copy(r_refs[a], out_refs[a].at[c], local_sems.at[a])
            cp.start()
            mine.append(cp)
            cp = pltpu.make_async_remote_copy(
                src_ref=r_refs[a], dst_ref=out_refs[a].at[c], send_sem=send_sems.at[a], recv_sem=recv_sems.at[a],
                device_id=(x, y, 1 - c), device_id_type=MESH)
            cp.start()
            sends.append(cp)
        for a in range(na):
            got = out_refs[a].at[1 - c]
            pltpu.make_async_remote_copy(src_ref=got, dst_ref=got, send_sem=send_sems.at[a], recv_sem=recv_sems.at[a],
                                         device_id=(x, y, c), device_id_type=MESH).wait_recv()
        for cp in sends:
            cp.wait_send()
        for cp in mine:
            cp.wait()

    return pl.pallas_call(
        body, name=name, in_specs=[_HBM] * na, out_specs=[_HBM] * na,
        out_shape=[jax.ShapeDtypeStruct((2,) + r.shape, r.dtype) for r in reds],
        scratch_shapes=[pltpu.SemaphoreType.DMA((na,)), pltpu.SemaphoreType.DMA((na,)),
                        pltpu.SemaphoreType.DMA((na,))],
    )(*reds)


def allgather_small(v, *, name):
    r, cols = v.shape

    def body(v_ref, out_ref, send_sems, recv_sems):
        x, y, c, _ = _place()
        me = 4 * x + 2 * y + c
        out_ref[me] = v_ref[...]
        peers = []
        for m in range(1, N_DEV):
            px = 1 - x if m & 4 else x
            py = 1 - y if m & 2 else y
            pc = 1 - c if m & 1 else c
            peers.append((px, py, pc))
        sends = []
        for k, peer in enumerate(peers):
            cp = pltpu.make_async_remote_copy(src_ref=v_ref, dst_ref=out_ref.at[me], send_sem=send_sems.at[k],
                                              recv_sem=recv_sems.at[k], device_id=peer, device_id_type=MESH)
            cp.start()
            sends.append(cp)
        for k, (px, py, pc) in enumerate(peers):
            got = out_ref.at[4 * px + 2 * py + pc]
            pltpu.make_async_remote_copy(src_ref=got, dst_ref=got, send_sem=send_sems.at[k], recv_sem=recv_sems.at[k],
                                         device_id=(x, y, c), device_id_type=MESH).wait_recv()
        for cp in sends:
            cp.wait_send()

    return pl.pallas_call(
        body, name=name, in_specs=[_VMEM], out_specs=_VMEM,
        out_shape=jax.ShapeDtypeStruct((N_DEV, r, cols), v.dtype),
        scratch_shapes=[pltpu.SemaphoreType.DMA((N_DEV - 1,)), pltpu.SemaphoreType.DMA((N_DEV - 1,))],
        compiler_params=pltpu.CompilerParams(vmem_limit_bytes=VMEM_LIMIT_BYTES),
    )(v)


def _slot_rows(r, cols, n_blocks):
    budget = max(16, (16 * 1024 * 1024) // (2 * n_blocks * 2 * cols))
    for cand in (1024, 512, 256, 128, 64, 32, 16):
        if cand <= budget and r % cand == 0:
            return cand
    return r


def add_slots(a, b, *, name):
    s, r, cols = a.shape
    tr = _slot_rows(r, cols, 3)

    def body(a_ref, b_ref, o_ref):
        o_ref[...] = (a_ref[...].astype(F32) + b_ref[...].astype(F32)).astype(o_ref.dtype)

    spec = pl.BlockSpec((None, tr, cols), lambda q, i: (q, i, 0))
    return pl.pallas_call(
        body, name=name, grid=(s, r // tr), in_specs=[spec, spec], out_specs=spec,
        out_shape=jax.ShapeDtypeStruct((s, r, cols), BF16), compiler_params=_cparams(("parallel", "parallel")),
    )(a, b)


def sum_slots(a, *, name):
    s, r, cols = a.shape
    tr = _slot_rows(r, cols, s + 2)

    def body(a_ref, o_ref):
        acc = a_ref[0].astype(F32)
        for q in range(1, s):
            acc = acc + a_ref[q].astype(F32)
        o_ref[...] = acc

    return pl.pallas_call(
        body, name=name, grid=(r // tr,), in_specs=[pl.BlockSpec((s, tr, cols), lambda i: (0, i, 0))],
        out_specs=pl.BlockSpec((tr, cols), lambda i: (i, 0)),
        out_shape=jax.ShapeDtypeStruct((r, cols), F32), compiler_params=_cparams(("parallel",)),
    )(a)


def reduce_scatter_layer(gs, *, tag):
    c = lax.axis_index("c")
    theirs = sibling_swap_halves(gs, name=f"rs_swap_{tag}")
    parts = []
    for a, (g, th) in enumerate(zip(gs, theirs)):
        rh = g.shape[1] // 2
        own = lax.dynamic_slice_in_dim(g, c * rh, rh, axis=1)
        parts.append(add_slots(own, th, name=f"rs_add{a}_{tag}"))
    landed = chip_scatter(parts, name=f"rs_scatter_{tag}")
    reds = [sum_slots(p, name=f"rs_sum{a}_{tag}") for a, p in enumerate(landed)]
    shared = sibling_share(reds, name=f"rs_share_{tag}")
    return [sh.reshape(g.shape[1], g.shape[2]) for sh, g in zip(shared, gs)]


def adamw(w, g, m, v, *, name):
    r, cols = w.shape
    budget_rows = max(8, (36 * 1024 * 1024) // (14 * 4 * max(cols, LANES)))
    tr = r
    if r % 8 == 0:
        for cand in (1024, 512, 256, 128, 64, 32, 16, 8):
            if cand <= budget_rows and r % cand == 0:
                tr = cand
                break

    def body(w_ref, g_ref, m_ref, v_ref, d_ref, nm_ref, nv_ref):
        gv = g_ref[...]
        nm = ADAM_B1 * m_ref[...] + (1.0 - ADAM_B1) * gv
        nv = ADAM_B2 * v_ref[...] + (1.0 - ADAM_B2) * jnp.square(gv)
        m_hat = nm / (1.0 - ADAM_B1 ** ADAM_STEP)
        v_hat = nv / (1.0 - ADAM_B2 ** ADAM_STEP)
        d_ref[...] = -ADAM_LR * (m_hat / (jnp.sqrt(v_hat) + ADAM_EPS) + ADAM_WD * w_ref[...])
        nm_ref[...] = nm
        nv_ref[...] = nv

    spec = pl.BlockSpec((tr, cols), lambda i: (i, 0))
    shp = jax.ShapeDtypeStruct((r, cols), F32)
    return pl.pallas_call(
        body, name=name, grid=(r // tr,), in_specs=[spec] * 4, out_specs=[spec] * 3, out_shape=[shp] * 3,
        compiler_params=_cparams(("parallel",)),
    )(w, g, m, v)


def adamw_nd(w, g, m, v, *, name):
    shape = w.shape
    two = (lambda a: a.reshape(1, -1)) if w.ndim == 1 else (lambda a: a.reshape(-1, shape[-1]))
    d, nm, nv = adamw(two(w), two(g), two(m), two(v), name=name)
    return d.reshape(shape), nm.reshape(shape), nv.reshape(shape)


def _cols_full(w4):
    return w4.transpose(1, 0, 2).reshape(w4.shape[1], -1)


def _cols_sharded(w):
    return w.reshape(w.shape[0], N_CHIPS, -1).transpose(1, 0, 2)


def _uq_pad(w):
    return jnp.pad(w.reshape(MLA_QL, MLA_H, MLA_NOPE + MLA_ROPE),
                   ((0, 0), (0, 0), (0, MLA_DQK - MLA_NOPE - MLA_ROPE))).reshape(MLA_QL, MLA_CAT_W)


def _uq_unpad(g):
    return g.reshape(MLA_QL, MLA_H, MLA_DQK)[:, :, :MLA_NOPE + MLA_ROPE].reshape(MLA_QL, -1)


def _ukv_split(w):
    return w.reshape(MLA_KVL, MLA_H, 2, MLA_NOPE).transpose(0, 2, 1, 3).reshape(MLA_KVL, -1)


def _ukv_unsplit(g):
    return g.reshape(MLA_KVL, 2, MLA_H, MLA_NOPE).transpose(0, 2, 1, 3).reshape(MLA_KVL, -1)


def _rows(v):
    n = v.shape[0]
    return jnp.pad(v, (0, -n % 1024)).reshape(-1, LANES)


def _fox_bias_views(c):
    ct = c[:, :FOX_H].T
    return ct[:, :, None], ct[:, None, :]


def layer_fwd(x, w, s, tabs, lay, li):
    d = lay.d
    n = lambda k: f"{k}_l{li}"
    sv = {"x": x}
    h1 = rmsnorm_fwd(x, s["g1"], name=n("norm1"))
    proj = mm_nn(h1, w["win"], name=n("proj"))
    ff = mm_nn(h1, w["win"], name=n("proj_ff"), out_dtype=F32, ncols=LANES, col0=lay.ff // LANES)
    c = fox_cumsum(ff, s["bf"], name=n("fox_c"))
    c_col, c_row = _fox_bias_views(c)
    a, lse_a = attention_fwd("fox", proj, proj, proj, name=n("fox_att"), h_n=FOX_H, dk=FOX_DH, dv=FOX_DH,
                             qcb=lay.fq // FOX_DH, kcb=lay.fk // FOX_DH, vcb=lay.fv // FOX_DH,
                             scale=FOX_DH ** -0.5, extra=(c_col, c_row))
    qn = rmsnorm_fwd(proj, s["gq"], name=n("mla_qnorm"), width=MLA_QL, cb=lay.mq // MLA_QL)
    kvn = rmsnorm_fwd(proj, s["gkv"], name=n("mla_kvnorm"), width=MLA_KVL, cb=lay.mkv // MLA_KVL)
    q_pre = mm_nn(qn, w["wuq"], name=n("mla_uq"))
    kv = mm_nn(kvn, w["wukv"], name=n("mla_ukv"))
    q_cat = rope_apply(q_pre, *tabs["mq"], name=n("mla_qrope"), half=MLA_ROPE // 2, groups=MLA_H, width=MLA_CAT_W)
    kr = rope_apply(proj, *tabs["mk"], name=n("mla_krope"), half=MLA_ROPE // 2, groups=1, width=LANES,
                    cb=lay.mkr // LANES)
    k_cat = mla_kcat(kv, kr, name=n("mla_kcat"))
    bm, lse_b = attention_fwd("mla", q_cat, k_cat, kv, name=n("mla_att"), h_n=MLA_H, dk=MLA_DQK, dv=MLA_V,
                              qcb=0, kcb=0, vcb=MLA_H, scale=(MLA_NOPE + MLA_ROPE) ** -0.5)
    rq = rope_apply(proj, *tabs["r"], name=n("ret_qrope"), half=RET_DK // 2, groups=RET_H, width=RET_QK_W,
                    cb=lay.rq // RET_QK_W)
    rk = rope_apply(proj, *tabs["r"], name=n("ret_krope"), half=RET_DK // 2, groups=RET_H, width=RET_QK_W,
                    cb=lay.rk // RET_QK_W, scale=RET_DK ** -0.5)
    on, rstd = attention_fwd("ret", rq, rk, proj, name=n("ret_att"), h_n=RET_H, dk=RET_DK, dv=RET_DV,
                             qcb=0, kcb=0, vcb=lay.rv // RET_DV, extra=(tabs["lgam"],))
    cc = ret_gate(on, proj, lay.rg // RET_V_W, name=n("ret_gate"))
    m0 = mm_nn(a, w["wf"], name=n("br_fox"))
    m1 = mm_nn(bm, w["wm"], name=n("br_mla"))
    m2 = mm_nn(cc, w["wr"], name=n("br_ret"))
    merged = merge_fwd(m0, m1, m2, proj, d, name=n("merge"))
    x1 = mm_nn(merged, w["wo"], name=n("out_proj"), out_dtype=F32, res=x)
    h2 = rmsnorm_fwd(x1, s["g2"], name=n("norm2"))
    u = mm_nn(h2, w["wup"], name=n("ffn_up"))
    gt = mm_nn(h2, w["wg"], name=n("ffn_gate"))
    act = ffn_act(u, gt, s["cw"], s["cb"], name=n("ffn_act"))
    x2 = mm_nn(act, w["wd"], name=n("ffn_down"), out_dtype=F32, res=x1)
    sv.update(h1=h1, proj=proj, ff=ff, c_col=c_col, c_row=c_row, a=a, lse_a=lse_a, qn=qn, kvn=kvn, q_cat=q_cat,
              k_cat=k_cat, kv=kv, bm=bm, lse_b=lse_b, rq=rq, rk=rk, on=on, rstd=rstd, cc=cc, m0=m0, m1=m1, m2=m2,
              merged=merged, x1=x1, h2=h2, u=u, gt=gt, act=act)
    return x2, sv


def layer_bwd(dx2, w, s, tabs, lay, sv, li):
    d = lay.d
    n = lambda k: f"{k}_l{li}"
    proj = sv["proj"]
    dact = mm_nt(dx2, w["wd"], name=n("d_act"))
    dwd = mm_tn(sv["act"], dx2, name=n("dw_down"))
    duc, dgt, dcw8, dcb = ffn_act_bwd(dact, sv["u"], sv["gt"], s["cw"], s["cb"], name=n("ffn_act_bwd"))
    du = conv3_transpose(duc, s["cw"], name=n("ffn_convT"))
    dh2 = mm_nt(du, w["wup"], name=n("d_h2_up"), out_dtype=F32)
    dh2 = mm_nt(dgt, w["wg"], name=n("d_h2_gate"), out_dtype=F32, res=dh2)
    dwup = mm_tn(sv["h2"], du, name=n("dw_up"), shards=N_CHIPS)
    dwg = mm_tn(sv["h2"], dgt, name=n("dw_gate"), shards=N_CHIPS)
    dx1, dg2 = rmsnorm_bwd(sv["x1"], s["g2"], dh2, name=n("norm2_bwd"), res=dx2)
    dmerged = mm_nt(dx1, w["wo"], name=n("d_merged"))
    dwo = mm_tn(sv["merged"], dx1, name=n("dw_out"))
    dm0, dm1, dm2, dgates = merge_bwd(dmerged, sv["m0"], sv["m1"], sv["m2"], proj, d, name=n("merge_bwd"))
    da = mm_nt(dm0, w["wf"], name=n("d_a"))
    dwf = mm_tn(sv["a"], dm0, name=n("dw_br_fox"), shards=N_CHIPS)
    dbm = mm_nt(dm1, w["wm"], name=n("d_bm"))
    dwm = mm_tn(sv["bm"], dm1, name=n("dw_br_mla"), shards=N_CHIPS)
    dcc = mm_nt(dm2, w["wr"], name=n("d_cc"))
    dwr = mm_tn(sv["cc"], dm2, name=n("dw_br_ret"), shards=N_CHIPS)
    drg, do_ret = ret_gate_bwd(dcc, sv["on"], proj, lay.rg // RET_V_W, sv["rstd"], name=n("ret_gate_bwd"))
    dq_r, dk_r, drv = attention_bwd("ret", sv["rq"], sv["rk"], proj, sv["on"], do_ret, sv["rstd"],
                                    name=n("ret_att_bwd"), h_n=RET_H, dk=RET_DK, dv=RET_DV, qcb=0, kcb=0,
                                    vcb=lay.rv // RET_DV, extra=(tabs["lgam"],))
    drq = rope_apply(dq_r, *tabs["r"], name=n("ret_qrope_bwd"), half=RET_DK // 2, groups=RET_H, width=RET_QK_W,
                     transpose=True)
    drk = rope_apply(dk_r, *tabs["r"], name=n("ret_krope_bwd"), half=RET_DK // 2, groups=RET_H, width=RET_QK_W,
                     transpose=True, scale=RET_DK ** -0.5)
    dq_cat, dk_cat, dv_m = attention_bwd("mla", sv["q_cat"], sv["k_cat"], sv["kv"], sv["bm"], dbm, sv["lse_b"],
                                         name=n("mla_att_bwd"), h_n=MLA_H, dk=MLA_DQK, dv=MLA_V, qcb=0, kcb=0,
                                         vcb=MLA_H, scale=(MLA_NOPE + MLA_ROPE) ** -0.5)
    dq_pre = rope_apply(dq_cat, *tabs["mq"], name=n("mla_qrope_bwd"), half=MLA_ROPE // 2, groups=MLA_H,
                        width=MLA_CAT_W, transpose=True)
    dkv, dkr = mla_kcat_bwd(dk_cat, dv_m, name=n("mla_kcat_bwd"))
    dmkr = rope_apply(dkr, *tabs["mk"], name=n("mla_krope_bwd"), half=MLA_ROPE // 2, groups=1, width=LANES,
                      transpose=True)
    dqn = mm_nt(dq_pre, w["wuq"], name=n("d_qn"), out_dtype=F32)
    dwuq = mm_tn(sv["qn"], dq_pre, name=n("dw_uq"))
    dkvn = mm_nt(dkv, w["wukv"], name=n("d_kvn"), out_dtype=F32)
    dwukv = mm_tn(sv["kvn"], dkv, name=n("dw_ukv"))
    dmq, dgq = rmsnorm_bwd(proj, s["gq"], dqn, name=n("mla_qnorm_bwd"), width=MLA_QL, cb=lay.mq // MLA_QL,
                           out_dtype=BF16)
    dmkv, dgkv = rmsnorm_bwd(proj, s["gkv"], dkvn, name=n("mla_kvnorm_bwd"), width=MLA_KVL, cb=lay.mkv // MLA_KVL,
                             out_dtype=BF16)
    dq_f, dk_f, dv_f, dck, dcq = attention_bwd("fox", proj, proj, proj, sv["a"], da, sv["lse_a"], name=n("fox_att_bwd"),
                                          h_n=FOX_H, dk=FOX_DH, dv=FOX_DH, qcb=lay.fq // FOX_DH,
                                          kcb=lay.fk // FOX_DH, vcb=lay.fv // FOX_DH, scale=FOX_DH ** -0.5,
                                          extra=(sv["c_col"], sv["c_row"]))
    dc = jnp.pad((dcq[:, :, 0] - dck[:, 0, :]).T, ((0, 0), (0, LANES - FOX_H)))
    dff, dbf = fox_cumsum_bwd(dc, sv["ff"], s["bf"], name=n("fox_c_bwd"))
    dproj = jnp.concatenate([dgates, drv, drg, dmq, drq, drk, dmkv, dq_f.astype(BF16), dk_f, dv_f, dff, dmkr], axis=1)
    dh1 = mm_nt(dproj, w["win"], name=n("d_h1"), out_dtype=F32)
    dwin = mm_tn(sv["h1"], dproj, name=n("dw_in"))
    dx, dg1 = rmsnorm_bwd(sv["x"], s["g1"], dh1, name=n("norm1_bwd"), res=dx1)
    big = [dwin.reshape(N_CHIPS, d // 4, lay.width),
           _cols_sharded(_uq_unpad(dwuq)), _cols_sharded(_ukv_unsplit(dwukv)), dwf, dwm, dwr,
           dwo.reshape(N_CHIPS, d // 4, d), dwup, dwg, dwd.reshape(N_CHIPS, -1, d)]
    small = [dg1[0], dgq[0], dgkv[0], dbf[0], dg2[0], dcw8[:3].reshape(-1), dcb[0]]
    return dx, big, small


def kernel(x, norm1_g, w_in, mla_q_norm_g, mla_kv_norm_g, mla_w_uq, mla_w_ukv, fox_b_f, w_br_fox, w_br_mla, w_br_ret, w_out, norm2_g, ffn_w_up, ffn_w_gate, ffn_conv_w, ffn_conv_b, ffn_w_down, final_norm_g, loss_target, m_norm1_g, m_w_in, m_mla_q_norm_g, m_mla_kv_norm_g, m_mla_w_uq, m_mla_w_ukv, m_fox_b_f, m_w_br_fox, m_w_br_mla, m_w_br_ret, m_w_out, m_norm2_g, m_ffn_w_up, m_ffn_w_gate, m_ffn_conv_w, m_ffn_conv_b, m_ffn_w_down, m_final_norm_g, v_norm1_g, v_w_in, v_mla_q_norm_g, v_mla_kv_norm_g, v_mla_w_uq, v_mla_w_ukv, v_fox_b_f, v_w_br_fox, v_w_br_mla, v_w_br_ret, v_w_out, v_norm2_g, v_ffn_w_up, v_ffn_w_gate, v_ffn_conv_w, v_ffn_conv_b, v_ffn_w_down, v_final_norm_g):
    names = ["norm1_g", "w_in", "mla_q_norm_g", "mla_kv_norm_g", "mla_w_uq", "mla_w_ukv", "fox_b_f", "w_br_fox",
             "w_br_mla", "w_br_ret", "w_out", "norm2_g", "ffn_w_up", "ffn_w_gate", "ffn_conv_w", "ffn_conv_b",
             "ffn_w_down", "final_norm_g"]
    wts = dict(zip(names, [norm1_g, w_in, mla_q_norm_g, mla_kv_norm_g, mla_w_uq, mla_w_ukv, fox_b_f, w_br_fox,
                           w_br_mla, w_br_ret, w_out, norm2_g, ffn_w_up, ffn_w_gate, ffn_conv_w, ffn_conv_b,
                           ffn_w_down, final_norm_g]))
    mom = dict(zip(names, [m_norm1_g, m_w_in, m_mla_q_norm_g, m_mla_kv_norm_g, m_mla_w_uq, m_mla_w_ukv, m_fox_b_f,
                           m_w_br_fox, m_w_br_mla, m_w_br_ret, m_w_out, m_norm2_g, m_ffn_w_up, m_ffn_w_gate,
                           m_ffn_conv_w, m_ffn_conv_b, m_ffn_w_down, m_final_norm_g]))
    var = dict(zip(names, [v_norm1_g, v_w_in, v_mla_q_norm_g, v_mla_kv_norm_g, v_mla_w_uq, v_mla_w_ukv, v_fox_b_f,
                           v_w_br_fox, v_w_br_mla, v_w_br_ret, v_w_out, v_norm2_g, v_ffn_w_up, v_ffn_w_gate,
                           v_ffn_conv_w, v_ffn_conv_b, v_ffn_w_down, v_final_norm_g]))
    t, d = x.shape[1], x.shape[2]
    depth = w_in.shape[0]
    f = ffn_conv_b.shape[1]
    fs = f // N_CHIPS
    lay = ProjLayout(d)
    chip = 2 * lax.axis_index("x") + lax.axis_index("y")

    tabs = {
        "r": rope_tables(t, RET_DK, RET_DK, 0),
        "mq": rope_tables(t, MLA_ROPE, MLA_DQK, MLA_NOPE),
        "mk": rope_tables(t, MLA_ROPE, LANES, 0),
        "lgam": jnp.log(1.0 - 2.0 ** (-5.0 - jnp.arange(RET_H, dtype=F32))).reshape(RET_H, 1, 1),
    }

    cw_n = depth * 3 * fs
    cw_all = allgather_small(_rows(ffn_conv_w.reshape(-1)), name="ag_conv_w")
    conv_w_full = (cw_all[0::2].reshape(N_CHIPS, -1)[:, :cw_n].reshape(N_CHIPS, depth, 3, fs)
                   .transpose(1, 2, 0, 3).reshape(depth, 3, f))

    def small_params(p):
        return {"g1": p["norm1_g"][None], "gq": p["mla_q_norm_g"][None], "gkv": p["mla_kv_norm_g"][None],
                "bf": jnp.pad(p["fox_b_f"], (0, LANES - FOX_H))[None], "g2": p["norm2_g"][None],
                "cw": p["conv_w_full"], "cb": p["ffn_conv_b"][None]}

    def gathered_weights(p):
        shards = [lay.pad_cols(p["w_in"]), p["mla_w_uq"], p["mla_w_ukv"], p["w_br_fox"], p["w_br_mla"],
                  p["w_br_ret"], p["w_out"], p["ffn_w_up"], p["ffn_w_gate"], p["ffn_w_down"]]
        win, wuq, wukv, wf, wm, wr, wo, wup, wg, wd = allgather_weights([s.astype(BF16) for s in shards],
                                                                        name="ag_layer")
        return {"win": win.reshape(d, lay.width), "wuq": _uq_pad(_cols_full(wuq)), "wukv": _ukv_split(_cols_full(wukv)),
                "wf": wf, "wm": wm, "wr": wr, "wo": wo.reshape(d, d), "wup": wup, "wg": wg, "wd": wd.reshape(f, d)}

    per_layer_in = {k: wts[k] for k in names if k != "final_norm_g"}
    per_layer_in["conv_w_full"] = conv_w_full

    def fwd_layer(xs, p):
        w, s = gathered_weights(p), small_params(p)
        xs, sv = layer_fwd(xs, w, s, tabs, lay, "f")
        return xs, (w, s, sv)

    xs, kept = x.reshape(t, d), []
    for li in range(depth):
        xs, kept_l = fwd_layer(xs, {k: v[li] for k, v in per_layer_in.items()})
        kept.append(kept_l)
    loss_vec, dx, dgf = loss_head(xs, final_norm_g[None], loss_target.reshape(t, d), name="loss_head")

    def bwd_layer(dx, kept_l):
        w, s, sv = kept_l
        dx, big, small = layer_bwd(dx, w, s, tabs, lay, sv, "b")
        return dx, (reduce_scatter_layer(big, tag="layer"), jnp.concatenate(small))

    reduced_l, smalls_l = [None] * depth, [None] * depth
    for li in reversed(range(depth)):
        dx, (reduced_l[li], smalls_l[li]) = bwd_layer(dx, kept[li])
    reduced = [jnp.stack([r[a] for r in reduced_l]) for a in range(len(reduced_l[0]))]
    smalls = jnp.stack(smalls_l)

    sizes = [d, MLA_QL, MLA_KVL, LANES, d, 3 * f, f]
    vec = jnp.concatenate([smalls.reshape(-1), dgf[0], loss_vec[0]])
    tot = sum_slots(allgather_small(_rows(vec), name="ag_small"), name="sum_small").reshape(-1)
    per_layer = sum(sizes)
    sm = {k: [] for k in ("g1", "gq", "gkv", "bf", "g2", "cw", "cb")}
    for li in range(depth):
        off = li * per_layer
        for k, sz in zip(("g1", "gq", "gkv", "bf", "g2", "cw", "cb"), sizes):
            sm[k].append(tot[off:off + sz])
            off += sz
    g_final = tot[depth * per_layer:depth * per_layer + d]
    loss = tot[depth * per_layer + d]

    gwin, guq, gukv, gwf, gwm, gwr, gwo, gwup, gwg, gwd = reduced
    grads = {
        "norm1_g": jnp.stack(sm["g1"]), "w_in": lay.unpad_cols(gwin), "mla_q_norm_g": jnp.stack(sm["gq"]),
        "mla_kv_norm_g": jnp.stack(sm["gkv"]), "mla_w_uq": guq, "mla_w_ukv": gukv,
        "fox_b_f": jnp.stack(sm["bf"])[:, :FOX_H], "w_br_fox": gwf, "w_br_mla": gwm, "w_br_ret": gwr, "w_out": gwo,
        "norm2_g": jnp.stack(sm["g2"]), "ffn_w_up": gwup, "ffn_w_gate": gwg,
        "ffn_conv_w": lax.dynamic_slice_in_dim(jnp.stack(sm["cw"]).reshape(depth, 3, f), chip * fs, fs, axis=2),
        "ffn_conv_b": jnp.stack(sm["cb"]), "ffn_w_down": gwd, "final_norm_g": g_final,
    }
    deltas, new_m, new_v = [], [], []
    for k in names:
        dl, nm, nv = adamw_nd(wts[k], grads[k], mom[k], var[k], name=f"adamw_{k}")
        deltas.append(dl)
        new_m.append(nm)
        new_v.append(nv)
    return (loss, dx.reshape(x.shape), *[grads[k] for k in names], *deltas, *new_m, *new_v)
```
